```python
import jax
import jax.numpy as jnp
from jax import lax
import numpy as np

D_MODEL = 1024
BATCH = 8
SEQ = 2048
DEPTH = 4

CTX_LEN = 256
GRID_W = 64
HEAD_DIM = 64
ROPE_THETA = 10000.0
EPS = 1e-6
Q_BLOCK = 128
N_BRANCH = 4
BRANCH_WIDTH = 256
A_HEADS = 4
A_KV_HEADS = 2
B_HEADS = 4
B_Q_RANK = 256
B_KV_RANK = 128
B_NOPE = 64
B_ROPE = 32
B_V = 64
C_HEADS = 4
C_KV_HEADS = 2
WINDOW = 128
WIN_BLOCK = 128
POOL_WINDOWS = (2, 4, 8, 16)
POOL_GROUPS = 4
POOL_DIM = 64
D_FF = ((8 * D_MODEL + 3 * 256 - 1) // (3 * 256)) * 256
IN_SPLITS = (A_HEADS * HEAD_DIM, A_KV_HEADS * HEAD_DIM, A_KV_HEADS * HEAD_DIM,
             B_Q_RANK, B_KV_RANK, B_ROPE,
             C_HEADS * HEAD_DIM, C_KV_HEADS * HEAD_DIM, C_KV_HEADS * HEAD_DIM,
             POOL_GROUPS * POOL_DIM, N_BRANCH * D_MODEL)
D_IN = sum(IN_SPLITS)

kernel_name = 'hybrid_prefix_diffusion_block'


def rmsnorm(x, g):
    xf = x.astype(jnp.float32)
    y = xf * lax.rsqrt(jnp.mean(xf * xf, axis=-1, keepdims=True) + EPS)
    return (y * g.astype(jnp.float32)).astype(x.dtype)


def modulate(h, shift, scale):
    return h * (1.0 + scale) + shift


def axial_rope(n_tokens, rot_dim, dtype):
    rows = n_tokens // GRID_W
    row = jnp.repeat(jnp.arange(rows), GRID_W).astype(jnp.float32)
    col = jnp.tile(jnp.arange(GRID_W), rows).astype(jnp.float32)
    n_freq = rot_dim // 4
    inv = ROPE_THETA ** (-jnp.arange(n_freq, dtype=jnp.float32) / n_freq)
    ang = jnp.concatenate([row[:, None] * inv, col[:, None] * inv], axis=-1)
    return jnp.cos(ang).astype(dtype), jnp.sin(ang).astype(dtype)


def apply_rope(x, cos, sin):
    x1, x2 = jnp.split(x, 2, axis=-1)
    return jnp.concatenate([x1 * cos - x2 * sin, x1 * sin + x2 * cos], axis=-1)


def to_heads(z, n):
    b, t, _ = z.shape
    return z.reshape(b, t, n, -1).transpose(0, 2, 1, 3)


def group_q(q, n_kv):
    b, h, t, d = q.shape
    return q.reshape(b, n_kv, h // n_kv, t, d)


def merge_heads(o):
    b, hk, g, t, d = o.shape
    return o.transpose(0, 3, 1, 2, 4).reshape(b, t, hk * g * d)


def split_in(z):
    parts, off = [], 0
    for n in IN_SPLITS:
        parts.append(z[..., off:off + n])
        off += n
    return parts


def dense_attention(q, k, v, scale):
    b, hk, g, t, d = q.shape
    nb = t // Q_BLOCK
    qb = jnp.moveaxis(q.reshape(b, hk, g, nb, Q_BLOCK, d), 3, 0)

    def one_block(qi):
        s = jnp.einsum('bkgqd,bktd->bkgqt', qi, k, preferred_element_type=jnp.float32) * scale
        p = jax.nn.softmax(s, axis=-1).astype(v.dtype)
        return jnp.einsum('bkgqt,bktv->bkgqv', p, v)

    o = lax.map(one_block, qb)
    return jnp.moveaxis(o, 0, 3).reshape(b, hk, g, t, v.shape[-1])


def sink_attention(q, k, v, sink, scale):
    b, hk, g, t, d = q.shape
    s = jnp.einsum('bkgqd,bktd->bkgqt', q, k, preferred_element_type=jnp.float32) * scale
    s_sink = jnp.broadcast_to(sink.astype(jnp.float32).reshape(1, hk, g, 1, 1), s.shape[:-1] + (1,))
    p = jax.nn.softmax(jnp.concatenate([s, s_sink], axis=-1), axis=-1)[..., :-1].astype(v.dtype)
    return jnp.einsum('bkgqt,bktd->bkgqd', p, v)


def window_sink_attention(q, k, v, kc, vc, sink, scale):
    b, hk, g, t, d = q.shape
    w = WIN_BLOCK
    nb = t // w

    def band(z):
        zp = jnp.pad(z, ((0, 0), (0, 0), (w, w), (0, 0))).reshape(b, hk, nb + 2, w, z.shape[-1])
        return jnp.concatenate([zp[:, :, :-2], zp[:, :, 1:-1], zp[:, :, 2:]], axis=3)

    kb, vb = band(k), band(v)
    qb = q.reshape(b, hk, g, nb, w, d)
    s_loc = jnp.einsum('bkgnqd,bkntd->bkgnqt', qb, kb, preferred_element_type=jnp.float32) * scale
    qpos = jnp.arange(nb)[:, None] * w + jnp.arange(w)[None, :]
    kpos = (jnp.arange(nb)[:, None] - 1) * w + jnp.arange(3 * w)[None, :]
    rel = kpos[:, None, :] - qpos[:, :, None]
    valid = (jnp.abs(rel) <= WINDOW) & (kpos[:, None, :] >= 0) & (kpos[:, None, :] < t)
    s_loc = jnp.where(valid, s_loc, -1e30)
    s_ctx = jnp.einsum('bkgnqd,bkcd->bkgnqc', qb, kc, preferred_element_type=jnp.float32) * scale
    s_sink = jnp.broadcast_to(sink.astype(jnp.float32).reshape(1, hk, g, 1, 1, 1), s_loc.shape[:-1] + (1,))
    p = jax.nn.softmax(jnp.concatenate([s_loc, s_ctx, s_sink], axis=-1), axis=-1)
    p_loc = p[..., :3 * w].astype(v.dtype)
    p_ctx = p[..., 3 * w:3 * w + kc.shape[2]].astype(v.dtype)
    o = (jnp.einsum('bkgnqt,bkntd->bkgnqd', p_loc, vb)
         + jnp.einsum('bkgnqc,bkcd->bkgnqd', p_ctx, vc))
    return o.reshape(b, hk, g, t, d)


def multiscale_pool(u, w_pool, pool_scale):
    b, t, _ = u.shape
    ug = u.reshape(b, t, POOL_GROUPS, POOL_DIM)
    cs = jnp.cumsum(jnp.pad(ug.astype(jnp.float32), ((0, 0), (1, 0), (0, 0), (0, 0))), axis=1)
    pos = jnp.arange(t)
    means = []
    for gi, win in enumerate(POOL_WINDOWS):
        lo = win // 2
        hi = win - lo - 1
        start = jnp.clip(pos - lo, 0, t)
        end = jnp.clip(pos + hi + 1, 0, t)
        csg = cs[:, :, gi]
        means.append((csg[:, end] - csg[:, start]) / (end - start).astype(jnp.float32)[None, :, None])
    pooled = jnp.stack(means, axis=2).astype(u.dtype)
    mixed = jnp.einsum('btgc,gcd->btgd', pooled - ug, w_pool).reshape(b, t, -1)
    return mixed * pool_scale


def stream_heads(z, rope64, rope32, a_qn_g, a_kn_g, b_qa_g, b_kva_g, b_w_qb, b_w_kvb):
    qa, ka, va, qba, kvba, kbr, qc, kc, vc, u, gl = split_in(z)
    qa = rmsnorm(to_heads(qa, A_HEADS), a_qn_g)
    ka = rmsnorm(to_heads(ka, A_KV_HEADS), a_kn_g)
    va = to_heads(va, A_KV_HEADS)
    qb = to_heads(rmsnorm(qba, b_qa_g) @ b_w_qb, B_HEADS)
    qb_nope, qb_rope = qb[..., :B_NOPE], qb[..., B_NOPE:]
    kvb = to_heads(rmsnorm(kvba, b_kva_g) @ b_w_kvb, B_HEADS)
    kb_nope, vb = kvb[..., :B_NOPE], kvb[..., B_NOPE:]
    kbr = kbr[:, None]
    qc = to_heads(qc, C_HEADS)
    kc = to_heads(kc, C_KV_HEADS)
    vc = to_heads(vc, C_KV_HEADS)
    if rope64 is not None:
        qa, ka, qc, kc = [apply_rope(a, *rope64) for a in (qa, ka, qc, kc)]
        qb_rope = apply_rope(qb_rope, *rope32)
        kbr = apply_rope(kbr, *rope32)
    qb = jnp.concatenate([qb_nope, qb_rope], axis=-1)
    kb = jnp.concatenate([kb_nope, jnp.broadcast_to(kbr, kb_nope.shape[:-1] + (B_ROPE,))], axis=-1)
    return (group_q(qa, A_KV_HEADS), ka, va, qb[:, :, None], kb, vb,
            group_q(qc, C_KV_HEADS), kc, vc, u, gl)


def merge_branches(outs, gl, w_branch, w_out):
    o = jnp.stack(outs, axis=2)
    proj = jnp.einsum('btkc,kcd->btkd', o, w_branch)
    gates = jax.nn.sigmoid(gl.reshape(gl.shape[:-1] + (N_BRANCH, D_MODEL)))
    return jnp.sum(gates * proj, axis=2) @ w_out


def mixer_layer(hx, hc, rope64, rope32, w_in, a_qn_g, a_kn_g, b_qa_g, b_kva_g, b_w_qb, b_w_kvb,
                c_sink, d_w_pool, d_scale, w_branch, w_out, with_ctx):
    sx = stream_heads(hx @ w_in, rope64, rope32, a_qn_g, a_kn_g, b_qa_g, b_kva_g, b_w_qb, b_w_kvb)
    sc = stream_heads(hc @ w_in, None, None, a_qn_g, a_kn_g, b_qa_g, b_kva_g, b_w_qb, b_w_kvb)
    qa_x, ka_x, va_x, qb_x, kb_x, vb_x, qc_x, kc_x, vc_x, u_x, gl_x = sx
    qa_c, ka_c, va_c, qb_c, kb_c, vb_c, qc_c, kc_c, vc_c, u_c, gl_c = sc
    sc_a = HEAD_DIM ** -0.5
    sc_b = (B_NOPE + B_ROPE) ** -0.5
    cat = lambda ctx_part, lat_part: jnp.concatenate([ctx_part, lat_part], axis=2)
    o_a = dense_attention(qa_x, cat(ka_c, ka_x), cat(va_c, va_x), sc_a)
    o_b = dense_attention(qb_x, cat(kb_c, kb_x), cat(vb_c, vb_x), sc_b)
    o_c = window_sink_attention(qc_x, kc_x, vc_x, kc_c, vc_c, c_sink, sc_a)
    o_d = multiscale_pool(u_x, d_w_pool, d_scale)
    yx = merge_branches([merge_heads(o_a), merge_heads(o_b), merge_heads(o_c), o_d], gl_x, w_branch, w_out)
    if not with_ctx:
        return yx, None
    oc_a = dense_attention(qa_c, ka_c, va_c, sc_a)
    oc_b = dense_attention(qb_c, kb_c, vb_c, sc_b)
    oc_c = sink_attention(qc_c, kc_c, vc_c, c_sink, sc_a)
    oc_d = multiscale_pool(u_c, d_w_pool, d_scale)
    yc = merge_branches([merge_heads(oc_a), merge_heads(oc_b), merge_heads(oc_c), oc_d], gl_c, w_branch, w_out)
    return yx, yc


def swiglu(h, w1, w3, w2):
    return (jax.nn.silu(h @ w1) * (h @ w3)) @ w2


def setup_inputs(seed: int = 0) -> dict:
    key = jax.random.key(seed)
    ks = iter(jax.random.split(key, 32))
    f32 = jnp.float32
    L = DEPTH

    def nrm(shape, fan_in):
        return jax.random.normal(next(ks), shape, f32) * fan_in ** -0.5

    def gain(shape, noise=0.02):
        return 1.0 + noise * jax.random.normal(next(ks), shape, f32)

    return {
        'x': jax.random.normal(next(ks), (BATCH, SEQ, D_MODEL), f32),
        'c': jax.random.normal(next(ks), (BATCH, D_MODEL), f32),
        'ctx': jax.random.normal(next(ks), (BATCH, CTX_LEN, D_MODEL), f32),
        'c_ctx': jax.random.normal(next(ks), (D_MODEL,), f32),
        'ada_w': nrm((L, D_MODEL, 6 * D_MODEL), D_MODEL),
        'ada_b': 0.02 * jax.random.normal(next(ks), (L, 6 * D_MODEL), f32),
        'mix_pre_g': gain((L, D_MODEL)),
        'mix_post_g': gain((L, D_MODEL)),
        'ffn_pre_g': gain((L, D_MODEL)),
        'ffn_post_g': gain((L, D_MODEL)),
        'w_in': nrm((L, D_MODEL, D_IN), D_MODEL),
        'a_qn_g': gain((L, HEAD_DIM)),
        'a_kn_g': gain((L, HEAD_DIM)),
        'b_qa_g': gain((L, B_Q_RANK)),
        'b_kva_g': gain((L, B_KV_RANK)),
        'b_w_qb': nrm((L, B_Q_RANK, B_HEADS * (B_NOPE + B_ROPE)), B_Q_RANK),
        'b_w_kvb': nrm((L, B_KV_RANK, B_HEADS * (B_NOPE + B_V)), B_KV_RANK),
        'c_sink': jax.random.normal(next(ks), (L, C_HEADS), f32),
        'd_w_pool': nrm((L, POOL_GROUPS, POOL_DIM, POOL_DIM), POOL_DIM),
        'd_scale': gain((L, POOL_GROUPS * POOL_DIM), 0.1),
        'w_branch': nrm((L, N_BRANCH, BRANCH_WIDTH, D_MODEL), BRANCH_WIDTH),
        'w_out': nrm((L, D_MODEL, D_MODEL), D_MODEL),
        'w_ffn1': nrm((L, D_MODEL, D_FF), D_MODEL),
        'w_ffn3': nrm((L, D_MODEL, D_FF), D_MODEL),
        'w_ffn2': nrm((L, D_FF, D_MODEL), D_FF),
    }


def reference(x, c, ctx, c_ctx, ada_w, ada_b, mix_pre_g, mix_post_g, ffn_pre_g, ffn_post_g,
              w_in, a_qn_g, a_kn_g, b_qa_g, b_kva_g, b_w_qb, b_w_kvb, c_sink, d_w_pool, d_scale,
              w_branch, w_out, w_ffn1, w_ffn3, w_ffn2):
    n_tok = x.shape[1]
    rope64 = axial_rope(n_tok, HEAD_DIM, x.dtype)
    rope32 = axial_rope(n_tok, B_ROPE, x.dtype)
    xc = ctx
    for l in range(DEPTH):
        with_ctx = l < DEPTH - 1
        mod_x = (jax.nn.silu(c) @ ada_w[l] + ada_b[l])[:, None, :]
        mod_c = (jax.nn.silu(c_ctx) @ ada_w[l] + ada_b[l])[None, None, :]
        shx1, scx1, gx1, shx2, scx2, gx2 = jnp.split(mod_x, 6, axis=-1)
        shc1, scc1, gc1, shc2, scc2, gc2 = jnp.split(mod_c, 6, axis=-1)
        hx = modulate(rmsnorm(x, mix_pre_g[l]), shx1, scx1)
        hc = modulate(rmsnorm(xc, mix_pre_g[l]), shc1, scc1)
        yx, yc = mixer_layer(hx, hc, rope64, rope32, w_in[l], a_qn_g[l], a_kn_g[l], b_qa_g[l], b_kva_g[l],
                             b_w_qb[l], b_w_kvb[l], c_sink[l], d_w_pool[l], d_scale[l], w_branch[l], w_out[l],
                             with_ctx)
        x = x + gx1 * rmsnorm(yx, mix_post_g[l])
        fx = swiglu(modulate(rmsnorm(x, ffn_pre_g[l]), shx2, scx2), w_ffn1[l], w_ffn3[l], w_ffn2[l])
        x = x + gx2 * rmsnorm(fx, ffn_post_g[l])
        if with_ctx:
            xc = xc + gc1 * rmsnorm(yc, mix_post_g[l])
            fc = swiglu(modulate(rmsnorm(xc, ffn_pre_g[l]), shc2, scc2), w_ffn1[l], w_ffn3[l], w_ffn2[l])
            xc = xc + gc2 * rmsnorm(fc, ffn_post_g[l])
    return x
```

```python
import functools

import numpy as np
import jax
import jax.numpy as jnp
from jax import lax
from jax.experimental import pallas as pl
from jax.experimental.pallas import tpu as pltpu

D_MODEL = 1024
DEPTH = 4
CTX_LEN = 256
GRID_W = 64
HEAD_DIM = 64
ROPE_THETA = 10000.0
EPS = 1e-6
N_BRANCH = 4
BRANCH_WIDTH = 256
B_HEADS = 4
B_Q_RANK = 256
B_KV_RANK = 128
B_NOPE = 64
B_ROPE = 32
B_V = 64
WINDOW = 128
POOL_WINDOWS = (2, 4, 8, 16)
POOL_GROUPS = 4
POOL_DIM = 64
D_FF = 2816

LANES = 128
TM = CTX_LEN
HALO = 64
BAND = 2 * TM
N_SMALL = 1792
VMEM_LIMIT = 56 * 1024 * 1024

F32 = jnp.float32
BF16 = jnp.bfloat16


def _dot(a, b):
    return jnp.dot(a, b, preferred_element_type=F32)


def _dot_t(a, b):
    return lax.dot_general(a, b, (((1,), (1,)), ((), ())), preferred_element_type=F32)


def _rms(x, g):
    return x * lax.rsqrt(jnp.mean(x * x, axis=-1, keepdims=True) + EPS) * g


def _lane_lo(shape, width):
    lane = lax.broadcasted_iota(jnp.int32, shape, 1)
    return (lane % (2 * width)) < width


def _swap_halves(x, half):
    return jnp.where(_lane_lo(x.shape, half),
                     pltpu.roll(x, LANES - half, 1), pltpu.roll(x, half, 1))


def _rope(x, cos, sin, half):
    return x * cos + _swap_halves(x, half) * sin


def _head_rms(x, g):
    lo = _lane_lo(x.shape, HEAD_DIM)
    s = x * x
    s_lo = jnp.sum(jnp.where(lo, s, 0.0), axis=-1, keepdims=True)
    s_hi = jnp.sum(jnp.where(lo, 0.0, s), axis=-1, keepdims=True)
    ms = jnp.where(lo, s_lo, s_hi) * (1.0 / HEAD_DIM)
    return x * lax.rsqrt(ms + EPS) * g


def _pair_layouts(k, v):
    lo = _lane_lo(k.shape, HEAD_DIM)
    kr = pltpu.roll(k, HEAD_DIM, 1)
    vr = pltpu.roll(v, HEAD_DIM, 1)
    return (jnp.where(lo, k, vr), jnp.where(lo, v, kr), jnp.where(lo, kr, v), jnp.where(lo, vr, k))


def _mods_kernel(c_ref, w_ref, b_ref, o_ref):
    c = c_ref[...]
    s = c / (1.0 + jnp.exp(-c))
    o_ref[...] = _dot(s.astype(BF16), w_ref[...].astype(BF16)) + b_ref[...]


def _mods_call(cs, ada_w, ada_b):
    depth, d, n = ada_w.shape
    nb = 1536
    return pl.pallas_call(
        _mods_kernel,
        grid=(depth, n // nb),
        in_specs=[pl.BlockSpec((16, d), lambda l, j: (0, 0)),
                  pl.BlockSpec((None, d, nb), lambda l, j: (l, 0, j)),
                  pl.BlockSpec((None, 1, nb), lambda l, j: (l, 0, j))],
        out_specs=pl.BlockSpec((None, 16, nb), lambda l, j: (l, 0, j)),
        out_shape=jax.ShapeDtypeStruct((depth, 16, n), F32),
        compiler_params=pltpu.CompilerParams(dimension_semantics=("arbitrary", "arbitrary"),
                                             vmem_limit_bytes=VMEM_LIMIT),
        name="adaln_mods",
    )(cs, ada_w, ada_b.reshape(depth, 1, n))


def _pre_kernel(x_ref, mod_ref, g_ref, w_ref, gq_ref, gk_ref, gqa_ref, gkva_ref, wqb_ref, wkvb_ref,
                cosa_ref, sina_ref, cosb_ref, sinb_ref,
                qa_ref, kva_ref, qb_ref, kb_ref, qc_ref, kvc_ref, u_ref):
    d = D_MODEL
    x = x_ref[...]
    h = _rms(x, g_ref[...]) * (1.0 + mod_ref[:, d:2 * d]) + mod_ref[:, 0:d]
    z = _dot(h.astype(BF16), w_ref[...])
    cosa, sina = cosa_ref[...], sina_ref[...]
    cosb, sinb = cosb_ref[...], sinb_ref[...]
    sc_a = HEAD_DIM ** -0.5
    sc_b = (B_NOPE + B_ROPE) ** -0.5

    for j in range(2):
        q = _head_rms(z[:, LANES * j:LANES * (j + 1)], gq_ref[...])
        qa_ref[:, LANES * j:LANES * (j + 1)] = (_rope(q, cosa, sina, 32) * sc_a).astype(BF16)
    k = _rope(_head_rms(z[:, 256:384], gk_ref[...]), cosa, sina, 32)
    for i, blk in enumerate(_pair_layouts(k, z[:, 384:512])):
        kva_ref[:, LANES * i:LANES * (i + 1)] = blk.astype(BF16)

    qn = _rms(z[:, 512:768], gqa_ref[...]).astype(BF16)
    qb = _dot(qn, wqb_ref[...]) * sc_b
    kvn = _rms(z[:, 768:896], gkva_ref[...]).astype(BF16)
    kvb = _dot(kvn, wkvb_ref[...])
    kbr = _rope(z[:, 896:1024], cosb, sinb, 16).astype(BF16)
    for hd in range(B_HEADS):
        c0 = 2 * LANES * hd
        qb_ref[:, c0:c0 + LANES] = qb[:, c0:c0 + LANES].astype(BF16)
        qb_ref[:, c0 + LANES:c0 + 2 * LANES] = _rope(qb[:, c0 + LANES:c0 + 2 * LANES], cosb, sinb, 16).astype(BF16)
        kb_ref[:, c0:c0 + LANES] = kvb[:, LANES * hd:LANES * (hd + 1)].astype(BF16)
        kb_ref[:, c0 + LANES:c0 + 2 * LANES] = kbr

    for j in range(2):
        q = z[:, 1024 + LANES * j:1024 + LANES * (j + 1)]
        qc_ref[:, LANES * j:LANES * (j + 1)] = (_rope(q, cosa, sina, 32) * sc_a).astype(BF16)
    k = _rope(z[:, 1280:1408], cosa, sina, 32)
    for i, blk in enumerate(_pair_layouts(k, z[:, 1408:1536])):
        kvc_ref[:, LANES * i:LANES * (i + 1)] = blk.astype(BF16)

    u_ref[...] = z[:, 1536:1792]


def _row_spec(width, t0=0):
    return pl.BlockSpec((None, TM, width), lambda b, t: (b, t + t0, 0))


def _mod_spec(t0=0):
    return pl.BlockSpec((None, None, 1, 6 * D_MODEL), lambda b, t: (b, jnp.minimum(t + t0, 1), 0, 0))


def _const_spec(shape):
    nd = len(shape)
    return pl.BlockSpec(shape, lambda b, t: (0,) * nd)


def _params():
    return pltpu.CompilerParams(dimension_semantics=("arbitrary", "arbitrary"),
                                vmem_limit_bytes=VMEM_LIMIT)


def _pre_call(xa, modt, g, w_small, gq, gk, gqa, gkva, wqb, wkvb, tabs):
    bsz, s, d = xa.shape
    nt = s // TM
    tab_spec = pl.BlockSpec((TM, LANES), lambda b, t: (t, 0))
    widths = (256, 512, 1024, 1024, 256, 512)
    out_shape = [jax.ShapeDtypeStruct((bsz, s, w), BF16) for w in widths]
    out_shape.append(jax.ShapeDtypeStruct((bsz, s, 256), F32))
    return pl.pallas_call(
        _pre_kernel,
        grid=(bsz, nt),
        in_specs=[_row_spec(d), _mod_spec(), _const_spec((1, d)), _const_spec(w_small.shape),
                  _const_spec((1, LANES)), _const_spec((1, LANES)), _const_spec((1, B_Q_RANK)),
                  _const_spec((1, B_KV_RANK)), _const_spec(wqb.shape), _const_spec(wkvb.shape),
                  tab_spec, tab_spec, tab_spec, tab_spec],
        out_specs=[_row_spec(w) for w in widths] + [_row_spec(256)],
        out_shape=out_shape,
        compiler_params=_params(),
        name="pre_attn",
    )(xa, modt, g, w_small, gq, gk, gqa, gkva, wqb, wkvb, *tabs)


def _softmax_pv(s, v, extra_logit=None):
    m = jnp.max(s, axis=-1, keepdims=True)
    if extra_logit is not None:
        m = jnp.maximum(m, extra_logit)
    p = jnp.exp(s - m)
    l = jnp.sum(p, axis=-1, keepdims=True)
    if extra_logit is not None:
        l = l + jnp.exp(extra_logit - m)
    return _dot(p.astype(BF16), v) / l


def _attn_a_body(q_ref, kv_ref, o_ref, nk):
    for j in range(2):
        q = q_ref[:, LANES * j:LANES * (j + 1)]
        lo = _lane_lo(q.shape, HEAD_DIM)
        zero = jnp.zeros_like(q)
        kv = kv_ref[0:nk, 2 * LANES * j:2 * LANES * j + LANES]
        vk = kv_ref[0:nk, 2 * LANES * j + LANES:2 * LANES * (j + 1)]
        o0 = _softmax_pv(_dot_t(jnp.where(lo, q, zero), kv), vk)
        o1 = _softmax_pv(_dot_t(jnp.where(lo, zero, q), vk), kv)
        o_ref[:, LANES * j:LANES * (j + 1)] = jnp.where(lo, o0, o1).astype(BF16)


def _attn_a_kernel(q_ref, kv_ref, o_ref, *, t0, s_len):
    if t0 == 0:
        t = pl.program_id(1)

        @pl.when(t == 0)
        def _():
            _attn_a_body(q_ref, kv_ref, o_ref, CTX_LEN)

        @pl.when(t > 0)
        def _():
            _attn_a_body(q_ref, kv_ref, o_ref, s_len)
    else:
        _attn_a_body(q_ref, kv_ref, o_ref, s_len)


def _attn_b_body(q_ref, kb_ref, o_ref, nk):
    outs = []
    for hd in range(B_HEADS):
        c0 = 2 * LANES * hd
        s = _dot_t(q_ref[:, c0:c0 + 2 * LANES], kb_ref[0:nk, c0:c0 + 2 * LANES])
        outs.append(_softmax_pv(s, kb_ref[0:nk, c0:c0 + LANES]))
    for j in range(2):
        lo = _lane_lo(outs[0].shape, HEAD_DIM)
        blk = jnp.where(lo, pltpu.roll(outs[2 * j], HEAD_DIM, 1), outs[2 * j + 1])
        o_ref[:, LANES * j:LANES * (j + 1)] = blk.astype(BF16)


def _attn_b_kernel(q_ref, kb_ref, o_ref, *, t0, s_len):
    if t0 == 0:
        t = pl.program_id(1)

        @pl.when(t == 0)
        def _():
            _attn_b_body(q_ref, kb_ref, o_ref, CTX_LEN)

        @pl.when(t > 0)
        def _():
            _attn_b_body(q_ref, kb_ref, o_ref, s_len)
    else:
        _attn_b_body(q_ref, kb_ref, o_ref, s_len)


def _attn_c_ctx(sink_ref, q_ref, kv_ref, o_ref):
    for j in range(2):
        q = q_ref[:, LANES * j:LANES * (j + 1)]
        lo = _lane_lo(q.shape, HEAD_DIM)
        zero = jnp.zeros_like(q)
        kv = kv_ref[0:CTX_LEN, 2 * LANES * j:2 * LANES * j + LANES]
        vk = kv_ref[0:CTX_LEN, 2 * LANES * j + LANES:2 * LANES * (j + 1)]
        o0 = _softmax_pv(_dot_t(jnp.where(lo, q, zero), kv), vk, sink_ref[2 * j])
        o1 = _softmax_pv(_dot_t(jnp.where(lo, zero, q), vk), kv, sink_ref[2 * j + 1])
        o_ref[:, LANES * j:LANES * (j + 1)] = jnp.where(lo, o0, o1).astype(BF16)


def _attn_c_latent(sink_ref, q_ref, kv_ref, o_ref, t, s_len):
    start = pl.multiple_of(jnp.minimum(t * TM - WINDOW, s_len - BAND), WINDOW)
    qpos = t * TM + lax.broadcasted_iota(jnp.int32, (TM, BAND), 0)
    kpos = start + lax.broadcasted_iota(jnp.int32, (TM, BAND), 1)
    valid = (jnp.abs(kpos - qpos) <= WINDOW) & (kpos >= CTX_LEN)

    def head(qm, k_band, v_band, k_ctx, v_ctx, sink):
        s_band = jnp.where(valid, _dot_t(qm, k_band), -1e30)
        s_ctx = _dot_t(qm, k_ctx)
        m = jnp.maximum(jnp.max(s_band, axis=-1, keepdims=True), jnp.max(s_ctx, axis=-1, keepdims=True))
        m = jnp.maximum(m, sink)
        p_band = jnp.exp(s_band - m)
        p_ctx = jnp.exp(s_ctx - m)
        l = (jnp.sum(p_band, axis=-1, keepdims=True) + jnp.sum(p_ctx, axis=-1, keepdims=True)
             + jnp.exp(sink - m))
        return (_dot(p_band.astype(BF16), v_band) + _dot(p_ctx.astype(BF16), v_ctx)) / l

    for j in range(2):
        q = q_ref[:, LANES * j:LANES * (j + 1)]
        lo = _lane_lo(q.shape, HEAD_DIM)
        zero = jnp.zeros_like(q)
        c_kv = slice(2 * LANES * j, 2 * LANES * j + LANES)
        c_vk = slice(2 * LANES * j + LANES, 2 * LANES * (j + 1))
        kv_b = kv_ref[pl.ds(start, BAND), c_kv]
        vk_b = kv_ref[pl.ds(start, BAND), c_vk]
        kv_c = kv_ref[0:CTX_LEN, c_kv]
        vk_c = kv_ref[0:CTX_LEN, c_vk]
        o0 = head(jnp.where(lo, q, zero), kv_b, vk_b, kv_c, vk_c, sink_ref[2 * j])
        o1 = head(jnp.where(lo, zero, q), vk_b, kv_b, vk_c, kv_c, sink_ref[2 * j + 1])
        o_ref[:, LANES * j:LANES * (j + 1)] = jnp.where(lo, o0, o1).astype(BF16)


def _attn_c_kernel(sink_ref, q_ref, kv_ref, o_ref, *, t0, s_len):
    t = pl.program_id(1) + t0
    if t0 == 0:
        @pl.when(t == 0)
        def _():
            _attn_c_ctx(sink_ref, q_ref, kv_ref, o_ref)

        @pl.when(t > 0)
        def _():
            _attn_c_latent(sink_ref, q_ref, kv_ref, o_ref, t, s_len)
    else:
        _attn_c_latent(sink_ref, q_ref, kv_ref, o_ref, t, s_len)


def _attn_call(kernel, name, q, kv, t0, sink=None):
    bsz, s, qw = q.shape
    nt = s // TM - t0
    kvw = kv.shape[-1]
    in_specs = [_row_spec(qw, t0), pl.BlockSpec((None, s, kvw), lambda b, t: (b, 0, 0))]
    args = [q, kv]
    if sink is not None:
        in_specs = [pl.BlockSpec(memory_space=pltpu.SMEM)] + in_specs
        args = [sink] + args
    return pl.pallas_call(
        functools.partial(kernel, t0=t0, s_len=s),
        grid=(bsz, nt),
        in_specs=in_specs,
        out_specs=_row_spec(BRANCH_WIDTH),
        out_shape=jax.ShapeDtypeStruct((bsz, nt * TM, BRANCH_WIDTH), BF16),
        compiler_params=_params(),
        name=name,
    )(*args)


def _split_bf16(x):
    hi = x.astype(BF16)
    return hi, (x - hi.astype(F32)).astype(BF16)


def _merge_kernel(x_ref, mod_ref, gpre_ref, wgl_ref, oa_ref, ob_ref, oc_ref,
                  u_ref, up_ref, un_ref, pm_ref, ph_ref, wpool_ref, dscale_ref,
                  wbr_ref, wout_ref, gpost_ref, out_ref, *, t0, nt_all):
    d = D_MODEL
    t = pl.program_id(1) + t0
    x = x_ref[...]
    hb = (_rms(x, gpre_ref[...]) * (1.0 + mod_ref[:, d:2 * d]) + mod_ref[:, 0:d]).astype(BF16)

    u = u_ref[...]
    prev_ok = t >= 2
    next_ok = (t >= 1) & (t <= nt_all - 2)
    halo = jnp.concatenate([jnp.where(prev_ok, up_ref[...], 0.0), jnp.where(next_ok, un_ref[...], 0.0)], axis=0)
    u_hi, u_lo = _split_bf16(u)
    h_hi, h_lo = _split_bf16(halo)
    lane = lax.broadcasted_iota(jnp.int32, (TM, 2 * LANES), 1)
    grp = lane // POOL_DIM
    sums = jnp.zeros((TM, 2 * LANES), F32)
    for gi in range(POOL_GROUPS):
        pm, ph = pm_ref[gi], ph_ref[gi]
        sg = _dot(pm, u_hi) + _dot(pm, u_lo) + _dot(ph, h_hi) + _dot(ph, h_lo)
        sums = jnp.where(grp == gi, sg, sums)
    row = lax.broadcasted_iota(jnp.int32, (TM, 2 * LANES), 0)
    pos = jnp.where(t == 0, row, (t - 1) * TM + row)
    n_tok = jnp.where(t == 0, CTX_LEN, (nt_all - 1) * TM)
    p_lo = jnp.left_shift(1, grp)
    cnt = jnp.minimum(pos + p_lo, n_tok) - jnp.maximum(pos - p_lo, 0)
    diff = sums / cnt.astype(F32) - u
    o_d = _dot(diff.astype(BF16), wpool_ref[...]) * dscale_ref[...]

    branches = (oa_ref[...], ob_ref[...], oc_ref[...], o_d.astype(BF16))
    acc = jnp.zeros((TM, d), F32)
    for k in range(N_BRANCH):
        gl = _dot(hb, wgl_ref[:, d * k:d * (k + 1)])
        proj = _dot(branches[k], wbr_ref[BRANCH_WIDTH * k:BRANCH_WIDTH * (k + 1), :])
        acc = acc + proj / (1.0 + jnp.exp(-gl))
    y = _dot(acc.astype(BF16), wout_ref[...])
    out_ref[...] = x + mod_ref[:, 2 * d:3 * d] * _rms(y, gpost_ref[...])


def _merge_call(xa, modt, gpre, wgl, oa, ob, oc, u, pm, ph, wpool, dscale, wbr, wout, gpost, t0):
    bsz, s, d = xa.shape
    nt_all = s // TM
    r = TM // HALO
    n_halo = s // HALO
    up_spec = pl.BlockSpec((None, HALO, 256), lambda b, t: (b, jnp.maximum((t + t0) * r - 1, 0), 0))
    un_spec = pl.BlockSpec((None, HALO, 256), lambda b, t: (b, jnp.minimum((t + t0 + 1) * r, n_halo - 1), 0))
    return pl.pallas_call(
        functools.partial(_merge_kernel, t0=t0, nt_all=nt_all),
        grid=(bsz, nt_all - t0),
        in_specs=[_row_spec(d, t0), _mod_spec(t0), _const_spec((1, d)), _const_spec(wgl.shape),
                  _row_spec(256), _row_spec(256), _row_spec(256),
                  _row_spec(256, t0), up_spec, un_spec, _const_spec(pm.shape), _const_spec(ph.shape),
                  _const_spec(wpool.shape), _const_spec((1, 256)),
                  _const_spec(wbr.shape), _const_spec(wout.shape), _const_spec((1, d))],
        out_specs=_row_spec(d),
        out_shape=jax.ShapeDtypeStruct((bsz, (nt_all - t0) * TM, d), F32),
        compiler_params=_params(),
        name="merge",
    )(xa, modt, gpre, wgl, oa, ob, oc, u, u, u, pm, ph, wpool, dscale, wbr, wout, gpost)


def _ffn_kernel(x_ref, mod_ref, gpre_ref, w1_ref, w3_ref, w2_ref, gpost_ref, out_ref):
    d = D_MODEL
    x = x_ref[...]
    hb = (_rms(x, gpre_ref[...]) * (1.0 + mod_ref[:, 4 * d:5 * d]) + mod_ref[:, 3 * d:4 * d]).astype(BF16)
    a = _dot(hb, w1_ref[...])
    b = _dot(hb, w3_ref[...])
    act = (a / (1.0 + jnp.exp(-a)) * b).astype(BF16)
    f = _dot(act, w2_ref[...])
    out_ref[...] = x + mod_ref[:, 5 * d:6 * d] * _rms(f, gpost_ref[...])


def _ffn_call(x1, modt, gpre, w1, w3, w2, gpost, t0):
    bsz, s, d = x1.shape
    return pl.pallas_call(
        _ffn_kernel,
        grid=(bsz, s // TM),
        in_specs=[_row_spec(d), _mod_spec(t0), _const_spec((1, d)), _const_spec(w1.shape),
                  _const_spec(w3.shape), _const_spec(w2.shape), _const_spec((1, d))],
        out_specs=_row_spec(d),
        out_shape=jax.ShapeDtypeStruct((bsz, s, d), F32),
        compiler_params=_params(),
        name="ffn",
    )(x1, modt, gpre, w1, w3, w2, gpost)


def _rope_tables(n_tok):
    rows = n_tok // GRID_W
    row = jnp.repeat(jnp.arange(rows), GRID_W).astype(F32)
    col = jnp.tile(jnp.arange(GRID_W), rows).astype(F32)

    def cos_sin(rot_dim):
        n_freq = rot_dim // 4
        inv = ROPE_THETA ** (-jnp.arange(n_freq, dtype=F32) / n_freq)
        ang = jnp.concatenate([row[:, None] * inv, col[:, None] * inv], axis=-1)
        return jnp.cos(ang), jnp.sin(ang)

    c64, s64 = cos_sin(HEAD_DIM)
    cos_a = jnp.tile(c64, (1, 4))
    sin_a = jnp.tile(jnp.concatenate([-s64, s64], axis=-1), (1, 2))
    c32, s32 = cos_sin(B_ROPE)
    pad = LANES - B_ROPE
    cos_b = jnp.concatenate([c32, c32, jnp.ones((n_tok, pad), F32)], axis=-1)
    sin_b = jnp.concatenate([-s32, s32, jnp.zeros((n_tok, pad), F32)], axis=-1)
    ident_c = jnp.ones((CTX_LEN, LANES), F32)
    ident_s = jnp.zeros((CTX_LEN, LANES), F32)
    return (jnp.concatenate([ident_c, cos_a]), jnp.concatenate([ident_s, sin_a]),
            jnp.concatenate([ident_c, cos_b]), jnp.concatenate([ident_s, sin_b]))


def _pool_matrices():
    i = np.arange(TM)[:, None]
    jm = np.arange(TM)[None, :]
    jh = np.arange(2 * HALO)[None, :]
    halo_pos = np.where(jh < HALO, jh - HALO, TM + jh - HALO)
    pm, ph = [], []
    for win in POOL_WINDOWS:
        lo, hi = win // 2, win - win // 2 - 1
        pm.append(((jm - i >= -lo) & (jm - i <= hi)).astype(np.float32))
        ph.append(((halo_pos - i >= -lo) & (halo_pos - i <= hi)).astype(np.float32))
    return jnp.asarray(np.stack(pm), BF16), jnp.asarray(np.stack(ph), BF16)


def _layer_weights(l, w_in, b_w_qb, b_w_kvb, d_w_pool):
    wl = w_in[l]
    d = wl.shape[0]
    zeros = lambda n: jnp.zeros((d, n), F32)
    w_small = jnp.concatenate([wl[:, 0:896], wl[:, 896:928], zeros(LANES - B_ROPE), wl[:, 928:1696]], axis=1)
    w_gl = wl[:, 1696:]
    qb_w = b_w_qb[l]
    zq = lambda n: jnp.zeros((qb_w.shape[0], n), F32)
    per_head = B_NOPE + B_ROPE
    cols = []
    for hd in range(B_HEADS):
        base = per_head * hd
        cols += [qb_w[:, base:base + B_NOPE], zq(LANES - B_NOPE),
                 qb_w[:, base + B_NOPE:base + per_head], zq(LANES - B_ROPE)]
    wqb = jnp.concatenate(cols, axis=1)
    wpool = jax.scipy.linalg.block_diag(*[d_w_pool[l, g] for g in range(POOL_GROUPS)])
    return w_small.astype(BF16), w_gl.astype(BF16), wqb.astype(BF16), b_w_kvb[l].astype(BF16), wpool.astype(BF16)


def kernel(x, c, ctx, c_ctx, ada_w, ada_b, mix_pre_g, mix_post_g, ffn_pre_g, ffn_post_g, w_in, a_qn_g, a_kn_g,
           b_qa_g, b_kva_g, b_w_qb, b_w_kvb, c_sink, d_w_pool, d_scale, w_branch, w_out, w_ffn1, w_ffn3, w_ffn2):
    bsz, n_tok, d = x.shape
    depth = ada_w.shape[0]
    assert d == D_MODEL and ctx.shape[1] == CTX_LEN and n_tok % TM == 0 and bsz + 1 <= 16

    cs = jnp.zeros((16, d), F32).at[:bsz].set(c).at[bsz].set(c_ctx)
    mods = _mods_call(cs, ada_w, ada_b)
    tabs = _rope_tables(n_tok)
    pm, ph = _pool_matrices()
    xa = jnp.concatenate([ctx, x], axis=1)
    row2 = lambda v: v.reshape(1, -1)
    tile2 = lambda v: jnp.tile(v, 2).reshape(1, -1)

    for l in range(depth):
        last = l == depth - 1
        t0 = 1 if last else 0
        modt = jnp.stack([jnp.broadcast_to(mods[l, bsz], (bsz, 6 * d)), mods[l, :bsz]], axis=1)[:, :, None, :]
        w_small, w_gl, wqb, wkvb, wpool = _layer_weights(l, w_in, b_w_qb, b_w_kvb, d_w_pool)
        qa, kva, qb, kb, qc, kvc, u = _pre_call(
            xa, modt, row2(mix_pre_g[l]), w_small, tile2(a_qn_g[l]), tile2(a_kn_g[l]),
            row2(b_qa_g[l]), row2(b_kva_g[l]), wqb, wkvb, tabs)
        oa = _attn_call(_attn_a_kernel, "attn_a", qa, kva, t0)
        ob = _attn_call(_attn_b_kernel, "attn_b", qb, kb, t0)
        oc = _attn_call(_attn_c_kernel, "attn_c", qc, kvc, t0, sink=c_sink[l])
        x1 = _merge_call(xa, modt, row2(mix_pre_g[l]), w_gl, oa, ob, oc, u, pm, ph, wpool, row2(d_scale[l]),
                         w_branch[l].reshape(N_BRANCH * BRANCH_WIDTH, d).astype(BF16), w_out[l].astype(BF16),
                         row2(mix_post_g[l]), t0)
        xa = _ffn_call(x1, modt, row2(ffn_pre_g[l]), w_ffn1[l].astype(BF16), w_ffn3[l].astype(BF16),
                       w_ffn2[l].astype(BF16), row2(ffn_post_g[l]), t0)
    return xa
```

```python
import functools

import numpy as np
import jax
import jax.numpy as jnp
from jax import lax
from jax.experimental import pallas as pl
from jax.experimental.pallas import tpu as pltpu

D_MODEL = 1024
DEPTH = 4
CTX_LEN = 256
GRID_W = 64
HEAD_DIM = 64
ROPE_THETA = 10000.0
EPS = 1e-6
N_BRANCH = 4
BRANCH_WIDTH = 256
B_HEADS = 4
B_Q_RANK = 256
B_KV_RANK = 128
B_NOPE = 64
B_ROPE = 32
B_V = 64
WINDOW = 128
POOL_WINDOWS = (2, 4, 8, 16)
POOL_GROUPS = 4
POOL_DIM = 64
D_FF = 2816
LOG2E = 1.4426950408889634

LANES = 128
TM = CTX_LEN
HALO = 64
BAND = 2 * TM
N_SMALL = 1792
VMEM_LIMIT = 56 * 1024 * 1024

F32 = jnp.float32
BF16 = jnp.bfloat16


def _dot(a, b):
    return jnp.dot(a, b, preferred_element_type=F32)


def _dot_t(a, b):
    return lax.dot_general(a, b, (((1,), (1,)), ((), ())), preferred_element_type=F32)


def _rms(x, g):
    return x * lax.rsqrt(jnp.mean(x * x, axis=-1, keepdims=True) + EPS) * g


def _lane_lo(shape, width):
    lane = lax.broadcasted_iota(jnp.int32, shape, 1)
    return (lane % (2 * width)) < width


def _swap_halves(x, half):
    return jnp.where(_lane_lo(x.shape, half),
                     pltpu.roll(x, LANES - half, 1), pltpu.roll(x, half, 1))


def _rope(x, cos, sin, half):
    return x * cos + _swap_halves(x, half) * sin


def _head_rms(x, g):
    lo = _lane_lo(x.shape, HEAD_DIM)
    s = x * x
    s_lo = jnp.sum(jnp.where(lo, s, 0.0), axis=-1, keepdims=True)
    s_hi = jnp.sum(jnp.where(lo, 0.0, s), axis=-1, keepdims=True)
    ms = jnp.where(lo, s_lo, s_hi) * (1.0 / HEAD_DIM)
    return x * lax.rsqrt(ms + EPS) * g


def _pair_layouts(k, v):
    lo = _lane_lo(k.shape, HEAD_DIM)
    kr = pltpu.roll(k, HEAD_DIM, 1)
    vr = pltpu.roll(v, HEAD_DIM, 1)
    return (jnp.where(lo, k, kr), jnp.where(lo, v, 1.0), jnp.where(lo, kr, k), jnp.where(lo, vr, 1.0))


def _mods_kernel(c_ref, w_ref, b_ref, o_ref):
    c = c_ref[...]
    s = c / (1.0 + jnp.exp(-c))
    o_ref[...] = _dot(s.astype(BF16), w_ref[...].astype(BF16)) + b_ref[...]


def _mods_call(cs, ada_w, ada_b):
    depth, d, n = ada_w.shape
    nb = 1536
    return pl.pallas_call(
        _mods_kernel,
        grid=(depth, n // nb),
        in_specs=[pl.BlockSpec((16, d), lambda l, j: (0, 0)),
                  pl.BlockSpec((None, d, nb), lambda l, j: (l, 0, j)),
                  pl.BlockSpec((None, 1, nb), lambda l, j: (l, 0, j))],
        out_specs=pl.BlockSpec((None, 16, nb), lambda l, j: (l, 0, j)),
        out_shape=jax.ShapeDtypeStruct((depth, 16, n), F32),
        compiler_params=pltpu.CompilerParams(dimension_semantics=("arbitrary", "arbitrary"),
                                             vmem_limit_bytes=VMEM_LIMIT),
        name="adaln_mods",
    )(cs, ada_w, ada_b.reshape(depth, 1, n))


def _pre_kernel(x_ref, mod_ref, g_ref, w_ref, gq_ref, gk_ref, gqa_ref, gkva_ref, wqb_ref, wkvb_ref,
                cosa_ref, sina_ref, cosb_ref, sinb_ref,
                qa_ref, kva_ref, qb_ref, kb_ref, vb_ref, qc_ref, kvc_ref, u_ref):
    d = D_MODEL
    x = x_ref[...]
    h = _rms(x, g_ref[...]) * (1.0 + mod_ref[:, d:2 * d]) + mod_ref[:, 0:d]
    z = _dot(h.astype(BF16), w_ref[...])
    cosa, sina = cosa_ref[...], sina_ref[...]
    cosb, sinb = cosb_ref[...], sinb_ref[...]
    sc_a = HEAD_DIM ** -0.5 * LOG2E
    sc_b = (B_NOPE + B_ROPE) ** -0.5 * LOG2E

    for j in range(2):
        q = _head_rms(z[:, LANES * j:LANES * (j + 1)], gq_ref[...])
        qa_ref[:, LANES * j:LANES * (j + 1)] = (_rope(q, cosa, sina, 32) * sc_a).astype(BF16)
    k = _rope(_head_rms(z[:, 256:384], gk_ref[...]), cosa, sina, 32)
    for i, blk in enumerate(_pair_layouts(k, z[:, 384:512])):
        kva_ref[:, LANES * i:LANES * (i + 1)] = blk.astype(BF16)

    qn = _rms(z[:, 512:768], gqa_ref[...]).astype(BF16)
    qb = _dot(qn, wqb_ref[...]) * sc_b
    kvn = _rms(z[:, 768:896], gkva_ref[...]).astype(BF16)
    kvb = _dot(kvn, wkvb_ref[...])
    kbr = _rope(z[:, 896:1024], cosb, sinb, 16)
    lo = _lane_lo(kbr.shape, HEAD_DIM)
    for hd in range(B_HEADS):
        c0 = LANES * hd
        qb_ref[:, c0:c0 + LANES] = _rope(qb[:, c0:c0 + LANES], cosb, sinb, 16).astype(BF16)
        kb_ref[:, c0:c0 + LANES] = (kvb[:, 2 * c0:2 * c0 + LANES] + kbr).astype(BF16)
        vb_ref[:, c0:c0 + LANES] = jnp.where(lo, kvb[:, 2 * c0 + LANES:2 * c0 + 2 * LANES], 1.0).astype(BF16)

    for j in range(2):
        q = z[:, 1024 + LANES * j:1024 + LANES * (j + 1)]
        qc_ref[:, LANES * j:LANES * (j + 1)] = (_rope(q, cosa, sina, 32) * sc_a).astype(BF16)
    k = _rope(z[:, 1280:1408], cosa, sina, 32)
    for i, blk in enumerate(_pair_layouts(k, z[:, 1408:1536])):
        kvc_ref[:, LANES * i:LANES * (i + 1)] = blk.astype(BF16)

    u_ref[...] = z[:, 1536:1792]


def _row_spec(width, t0=0):
    return pl.BlockSpec((None, TM, width), lambda b, t: (b, t + t0, 0))


def _mod_spec(t0=0):
    return pl.BlockSpec((None, None, 1, 6 * D_MODEL), lambda b, t: (b, jnp.minimum(t + t0, 1), 0, 0))


def _const_spec(shape):
    nd = len(shape)
    return pl.BlockSpec(shape, lambda b, t: (0,) * nd)


def _params():
    return pltpu.CompilerParams(dimension_semantics=("arbitrary", "arbitrary"),
                                vmem_limit_bytes=VMEM_LIMIT)


def _pre_call(xa, modt, g, w_small, gq, gk, gqa, gkva, wqb, wkvb, tabs):
    bsz, s, d = xa.shape
    nt = s // TM
    tab_spec = pl.BlockSpec((TM, LANES), lambda b, t: (t, 0))
    widths = (256, 512, 512, 512, 512, 256, 512)
    out_shape = [jax.ShapeDtypeStruct((bsz, s, w), BF16) for w in widths]
    out_shape.append(jax.ShapeDtypeStruct((bsz, s, 256), F32))
    return pl.pallas_call(
        _pre_kernel,
        grid=(bsz, nt),
        in_specs=[_row_spec(d), _mod_spec(), _const_spec((1, d)), _const_spec(w_small.shape),
                  _const_spec((1, LANES)), _const_spec((1, LANES)), _const_spec((1, B_Q_RANK)),
                  _const_spec((1, B_KV_RANK)), _const_spec(wqb.shape), _const_spec(wkvb.shape),
                  tab_spec, tab_spec, tab_spec, tab_spec],
        out_specs=[_row_spec(w) for w in widths] + [_row_spec(256)],
        out_shape=out_shape,
        compiler_params=_params(),
        name="pre_attn",
    )(xa, modt, g, w_small, gq, gk, gqa, gkva, wqb, wkvb, *tabs)


def _probs(scores, extra=None):
    m = functools.reduce(jnp.maximum, [jnp.max(s, axis=-1, keepdims=True) for s in scores])
    if extra is not None:
        m = jnp.maximum(m, extra)
    ps = [jnp.exp2(s - m).astype(BF16) for s in scores]
    return ps, (None if extra is None else jnp.exp2(extra - m))


def _normalise_pair(r0, r1, e0=None, e1=None):
    lo = _lane_lo(r0.shape, HEAD_DIM)
    r0r = pltpu.roll(r0, HEAD_DIM, 1)
    r1r = pltpu.roll(r1, HEAD_DIM, 1)
    den0 = r0r if e0 is None else r0r + e0
    den1 = r1 if e1 is None else r1 + e1
    return jnp.where(lo, r0 / den0, r1r / den1)


def _stack_pair(q):
    lo = _lane_lo(q.shape, HEAD_DIM)
    zero = jnp.zeros_like(q)
    return jnp.concatenate([jnp.where(lo, q, zero), jnp.where(lo, zero, q)], axis=0)


def _dense_attn_kernel(*refs, heads, n_kv, with_ctx):
    q_ref = refs[0]
    kv_refs = refs[1:1 + n_kv]
    o_ref = refs[1 + n_kv]
    s_bufs = refs[2 + n_kv:4 + n_kv]
    p_bufs = refs[4 + n_kv:6 + n_kv]
    r_bufs = refs[6 + n_kv:8 + n_kv]
    s_len = q_ref.shape[0]
    n_lat = (s_len - CTX_LEN) // TM
    n_heads = len(heads)

    def q_block(row0, h):
        qcol, mask, *_ = heads[h]
        q = q_ref[pl.ds(row0, TM), LANES * qcol:LANES * (qcol + 1)]
        if mask is None:
            return q
        lo = _lane_lo(q.shape, HEAD_DIM)
        return jnp.where(lo if mask == "lo" else ~lo, q, jnp.zeros_like(q))

    def keys(h, nk):
        _, _, kref, kcol, _, _ = heads[h]
        return kv_refs[kref][0:nk, LANES * kcol:LANES * (kcol + 1)]

    def values(h, nk):
        _, _, _, _, vref, vcol = heads[h]
        return kv_refs[vref][0:nk, LANES * vcol:LANES * (vcol + 1)]

    o_off = s_len - o_ref.shape[0]

    def finish_pair(row0, j, r0, r1):
        o_ref[pl.ds(row0 - o_off, TM), LANES * j:LANES * (j + 1)] = _normalise_pair(r0, r1).astype(BF16)

    if with_ctx:
        rs = []
        for h in range(n_heads):
            (p,), _ = _probs([_dot_t(q_block(0, h), keys(h, CTX_LEN))])
            rs.append(_dot(p, values(h, CTX_LEN)))
        for j in range(n_heads // 2):
            finish_pair(0, j, rs[2 * j], rs[2 * j + 1])

    def scores(row0, h, buf):
        s_bufs[buf][...] = _dot_t(q_block(row0, h), keys(h, s_len))

    def softmax(buf):
        (p,), _ = _probs([s_bufs[buf][...]])
        p_bufs[buf][...] = p

    def weighted(h, buf):
        return _dot(p_bufs[buf][...], values(h, s_len))

    p_bufs[1][...] = jnp.zeros(p_bufs[1].shape, BF16)
    for rb in r_bufs:
        rb[...] = jnp.zeros(rb.shape, F32)
    scores(CTX_LEN, 0, 0)

    def tile_body(i, carry):
        row0 = pl.multiple_of(CTX_LEN + i * TM, TM)
        prev0 = pl.multiple_of(CTX_LEN + jnp.maximum(i - 1, 0) * TM, TM)
        next0 = pl.multiple_of(CTX_LEN + jnp.minimum(i + 1, n_lat - 1) * TM, TM)
        for h in range(n_heads):
            buf = h % 2
            if h + 1 < n_heads:
                scores(row0, h + 1, 1 - buf)
            else:
                scores(next0, 0, 1 - buf)
            softmax(buf)
            hp = (h - 1) % n_heads
            r = weighted(hp, 1 - buf)
            if hp % 2 == 0:
                r_bufs[hp // 2][...] = r
            else:
                finish_pair(prev0 if h == 0 else row0, hp // 2, r_bufs[hp // 2][...], r)
        return carry

    lax.fori_loop(0, n_lat, tile_body, 0)
    last = n_heads - 1
    finish_pair(s_len - TM, last // 2, r_bufs[last // 2][...], weighted(last, last % 2))


_HEADS_A = tuple((h // 2, "lo" if h % 2 == 0 else "hi", 0, 2 * (h // 2), 0, 2 * (h // 2) + 1) for h in range(4))
_HEADS_B = tuple((h, None, 0, h, 1, h) for h in range(4))


def _dense_attn_call(name, q, kvs, heads, with_ctx):
    bsz, s, qw = q.shape
    rows = s if with_ctx else s - CTX_LEN
    whole = lambda w, n=s: pl.BlockSpec((None, n, w), lambda b: (b, 0, 0))
    return pl.pallas_call(
        functools.partial(_dense_attn_kernel, heads=heads, n_kv=len(kvs), with_ctx=with_ctx),
        grid=(bsz,),
        in_specs=[whole(qw)] + [whole(kv.shape[-1]) for kv in kvs],
        out_specs=whole(BRANCH_WIDTH, rows),
        out_shape=jax.ShapeDtypeStruct((bsz, rows, BRANCH_WIDTH), BF16),
        scratch_shapes=[pltpu.VMEM((TM, s), F32), pltpu.VMEM((TM, s), F32),
                        pltpu.VMEM((TM, s), BF16), pltpu.VMEM((TM, s), BF16),
                        pltpu.VMEM((TM, LANES), F32), pltpu.VMEM((TM, LANES), F32)],
        compiler_params=pltpu.CompilerParams(dimension_semantics=("arbitrary",), vmem_limit_bytes=VMEM_LIMIT),
        name=name,
    )(q, *kvs)


def _sink_column(sink_ref, j):
    row = lax.broadcasted_iota(jnp.int32, (2 * TM, 1), 0)
    return jnp.where(row < TM, sink_ref[2 * j], sink_ref[2 * j + 1]) * LOG2E


def _attn_c_ctx(sink_ref, q_ref, kv_ref, o_ref):
    for j in range(2):
        kk = kv_ref[0:CTX_LEN, 2 * LANES * j:2 * LANES * j + LANES]
        v1 = kv_ref[0:CTX_LEN, 2 * LANES * j + LANES:2 * LANES * (j + 1)]
        (p,), e = _probs([_dot_t(_stack_pair(q_ref[:, LANES * j:LANES * (j + 1)]), kk)], _sink_column(sink_ref, j))
        r = _dot(p, v1)
        o_ref[:, LANES * j:LANES * (j + 1)] = _normalise_pair(r[:TM], r[TM:], e[:TM], e[TM:]).astype(BF16)


def _attn_c_latent(sink_ref, q_ref, kv_ref, o_ref, t, s_len):
    start = pl.multiple_of(jnp.minimum(t * TM - WINDOW, s_len - BAND), WINDOW)
    row = lax.broadcasted_iota(jnp.int32, (2 * TM, BAND), 0)
    qpos = t * TM + jnp.where(row < TM, row, row - TM)
    kpos = start + lax.broadcasted_iota(jnp.int32, (2 * TM, BAND), 1)
    valid = (jnp.abs(kpos - qpos) <= WINDOW) & (kpos >= CTX_LEN)
    for j in range(2):
        c_kk = slice(2 * LANES * j, 2 * LANES * j + LANES)
        c_v1 = slice(2 * LANES * j + LANES, 2 * LANES * (j + 1))
        qs = _stack_pair(q_ref[:, LANES * j:LANES * (j + 1)])
        s_band = jnp.where(valid, _dot_t(qs, kv_ref[pl.ds(start, BAND), c_kk]), -1e30)
        s_ctx = _dot_t(qs, kv_ref[0:CTX_LEN, c_kk])
        (p_band, p_ctx), e = _probs([s_band, s_ctx], _sink_column(sink_ref, j))
        r = _dot(p_band, kv_ref[pl.ds(start, BAND), c_v1]) + _dot(p_ctx, kv_ref[0:CTX_LEN, c_v1])
        o_ref[:, LANES * j:LANES * (j + 1)] = _normalise_pair(r[:TM], r[TM:], e[:TM], e[TM:]).astype(BF16)


def _attn_c_kernel(sink_ref, q_ref, kv_ref, o_ref, *, t0, s_len):
    t = pl.program_id(1) + t0
    if t0 == 0:
        @pl.when(t == 0)
        def _():
            _attn_c_ctx(sink_ref, q_ref, kv_ref, o_ref)

        @pl.when(t > 0)
        def _():
            _attn_c_latent(sink_ref, q_ref, kv_ref, o_ref, t, s_len)
    else:
        _attn_c_latent(sink_ref, q_ref, kv_ref, o_ref, t, s_len)


def _attn_call(kernel, name, q, kvs, t0, sink=None):
    bsz, s, qw = q.shape
    nt = s // TM - t0
    in_specs = [_row_spec(qw, t0)] + [pl.BlockSpec((None, s, kv.shape[-1]), lambda b, t: (b, 0, 0)) for kv in kvs]
    args = [q, *kvs]
    if sink is not None:
        in_specs = [pl.BlockSpec(memory_space=pltpu.SMEM)] + in_specs
        args = [sink] + args
    return pl.pallas_call(
        functools.partial(kernel, t0=t0, s_len=s),
        grid=(bsz, nt),
        in_specs=in_specs,
        out_specs=_row_spec(BRANCH_WIDTH),
        out_shape=jax.ShapeDtypeStruct((bsz, nt * TM, BRANCH_WIDTH), BF16),
        compiler_params=_params(),
        name=name,
    )(*args)


def _split_bf16(x):
    hi = x.astype(BF16)
    return hi, (x - hi.astype(F32)).astype(BF16)


def _merge_kernel(x_ref, mod_ref, gpre_ref, wgl_ref, oa_ref, ob_ref, oc_ref,
                  u_ref, up_ref, un_ref, pm_ref, ph_ref, wpool_ref, dscale_ref,
                  wbr_ref, wout_ref, gpost_ref, out_ref, *, t0, nt_all):
    d = D_MODEL
    t = pl.program_id(1) + t0
    x = x_ref[...]
    hb = (_rms(x, gpre_ref[...]) * (1.0 + mod_ref[:, d:2 * d]) + mod_ref[:, 0:d]).astype(BF16)

    u = u_ref[...]
    prev_ok = t >= 2
    next_ok = (t >= 1) & (t <= nt_all - 2)
    halo = jnp.concatenate([jnp.where(prev_ok, up_ref[...], 0.0), jnp.where(next_ok, un_ref[...], 0.0)], axis=0)
    u_hi, u_lo = _split_bf16(u)
    h_hi, h_lo = _split_bf16(halo)
    lane = lax.broadcasted_iota(jnp.int32, (TM, 2 * LANES), 1)
    grp = lane // POOL_DIM
    sums = jnp.zeros((TM, 2 * LANES), F32)
    for gi in range(POOL_GROUPS):
        pm, ph = pm_ref[gi], ph_ref[gi]
        sg = _dot(pm, u_hi) + _dot(pm, u_lo) + _dot(ph, h_hi) + _dot(ph, h_lo)
        sums = jnp.where(grp == gi, sg, sums)
    row = lax.broadcasted_iota(jnp.int32, (TM, 2 * LANES), 0)
    pos = jnp.where(t == 0, row, (t - 1) * TM + row)
    n_tok = jnp.where(t == 0, CTX_LEN, (nt_all - 1) * TM)
    p_lo = jnp.left_shift(1, grp)
    cnt = jnp.minimum(pos + p_lo, n_tok) - jnp.maximum(pos - p_lo, 0)
    diff = sums / cnt.astype(F32) - u
    o_d = _dot(diff.astype(BF16), wpool_ref[...]) * dscale_ref[...]

    branches = (oa_ref[...], ob_ref[...], oc_ref[...], o_d.astype(BF16))
    acc = jnp.zeros((TM, d), F32)
    for k in range(N_BRANCH):
        gl = _dot(hb, wgl_ref[:, d * k:d * (k + 1)])
        proj = _dot(branches[k], wbr_ref[BRANCH_WIDTH * k:BRANCH_WIDTH * (k + 1), :])
        acc = acc + proj / (1.0 + jnp.exp(-gl))
    y = _dot(acc.astype(BF16), wout_ref[...])
    out_ref[...] = x + mod_ref[:, 2 * d:3 * d] * _rms(y, gpost_ref[...])


def _merge_call(xa, modt, gpre, wgl, oa, ob, oc, u, pm, ph, wpool, dscale, wbr, wout, gpost, t0):
    bsz, s, d = xa.shape
    nt_all = s // TM
    r = TM // HALO
    n_halo = s // HALO
    up_spec = pl.BlockSpec((None, HALO, 256), lambda b, t: (b, jnp.maximum((t + t0) * r - 1, 0), 0))
    un_spec = pl.BlockSpec((None, HALO, 256), lambda b, t: (b, jnp.minimum((t + t0 + 1) * r, n_halo - 1), 0))
    return pl.pallas_call(
        functools.partial(_merge_kernel, t0=t0, nt_all=nt_all),
        grid=(bsz, nt_all - t0),
        in_specs=[_row_spec(d, t0), _mod_spec(t0), _const_spec((1, d)), _const_spec(wgl.shape),
                  _row_spec(256), _row_spec(256), _row_spec(256),
                  _row_spec(256, t0), up_spec, un_spec, _const_spec(pm.shape), _const_spec(ph.shape),
                  _const_spec(wpool.shape), _const_spec((1, 256)),
                  _const_spec(wbr.shape), _const_spec(wout.shape), _const_spec((1, d))],
        out_specs=_row_spec(d),
        out_shape=jax.ShapeDtypeStruct((bsz, (nt_all - t0) * TM, d), F32),
        compiler_params=_params(),
        name="merge",
    )(xa, modt, gpre, wgl, oa, ob, oc, u, u, u, pm, ph, wpool, dscale, wbr, wout, gpost)


def _ffn_kernel(x_ref, mod_ref, gpre_ref, w1_ref, w3_ref, w2_ref, gpost_ref, out_ref):
    d = D_MODEL
    x = x_ref[...]
    hb = (_rms(x, gpre_ref[...]) * (1.0 + mod_ref[:, 4 * d:5 * d]) + mod_ref[:, 3 * d:4 * d]).astype(BF16)
    a = _dot(hb, w1_ref[...])
    b = _dot(hb, w3_ref[...])
    act = (a / (1.0 + jnp.exp(-a)) * b).astype(BF16)
    f = _dot(act, w2_ref[...])
    out_ref[...] = x + mod_ref[:, 5 * d:6 * d] * _rms(f, gpost_ref[...])


def _ffn_call(x1, modt, gpre, w1, w3, w2, gpost, t0):
    bsz, s, d = x1.shape
    return pl.pallas_call(
        _ffn_kernel,
        grid=(bsz, s // TM),
        in_specs=[_row_spec(d), _mod_spec(t0), _const_spec((1, d)), _const_spec(w1.shape),
                  _const_spec(w3.shape), _const_spec(w2.shape), _const_spec((1, d))],
        out_specs=_row_spec(d),
        out_shape=jax.ShapeDtypeStruct((bsz, s, d), F32),
        compiler_params=_params(),
        name="ffn",
    )(x1, modt, gpre, w1, w3, w2, gpost)


def _rope_tables(n_tok):
    rows = n_tok // GRID_W
    row = jnp.repeat(jnp.arange(rows), GRID_W).astype(F32)
    col = jnp.tile(jnp.arange(GRID_W), rows).astype(F32)

    def cos_sin(rot_dim):
        n_freq = rot_dim // 4
        inv = ROPE_THETA ** (-jnp.arange(n_freq, dtype=F32) / n_freq)
        ang = jnp.concatenate([row[:, None] * inv, col[:, None] * inv], axis=-1)
        return jnp.cos(ang), jnp.sin(ang)

    c64, s64 = cos_sin(HEAD_DIM)
    cos_a = jnp.tile(c64, (1, 4))
    sin_a = jnp.tile(jnp.concatenate([-s64, s64], axis=-1), (1, 2))
    c32, s32 = cos_sin(B_ROPE)
    ones = lambda n: jnp.ones((n_tok, n), F32)
    zeros = lambda n: jnp.zeros((n_tok, n), F32)
    pad = LANES - B_NOPE - B_ROPE
    cos_b = jnp.concatenate([ones(B_NOPE), c32, c32, ones(pad)], axis=-1)
    sin_b = jnp.concatenate([zeros(B_NOPE), -s32, s32, zeros(pad)], axis=-1)
    ident_c = jnp.ones((CTX_LEN, LANES), F32)
    ident_s = jnp.zeros((CTX_LEN, LANES), F32)
    return (jnp.concatenate([ident_c, cos_a]), jnp.concatenate([ident_s, sin_a]),
            jnp.concatenate([ident_c, cos_b]), jnp.concatenate([ident_s, sin_b]))


def _pool_matrices():
    i = np.arange(TM)[:, None]
    jm = np.arange(TM)[None, :]
    jh = np.arange(2 * HALO)[None, :]
    halo_pos = np.where(jh < HALO, jh - HALO, TM + jh - HALO)
    pm, ph = [], []
    for win in POOL_WINDOWS:
        lo, hi = win // 2, win - win // 2 - 1
        pm.append(((jm - i >= -lo) & (jm - i <= hi)).astype(np.float32))
        ph.append(((halo_pos - i >= -lo) & (halo_pos - i <= hi)).astype(np.float32))
    return jnp.asarray(np.stack(pm), BF16), jnp.asarray(np.stack(ph), BF16)


def _layer_weights(l, w_in, b_w_qb, b_w_kvb, d_w_pool):
    wl = w_in[l]
    d = wl.shape[0]
    zeros = lambda n: jnp.zeros((d, n), F32)
    w_small = jnp.concatenate([wl[:, 0:896], zeros(B_NOPE), wl[:, 896:928], zeros(LANES - B_NOPE - B_ROPE),
                               wl[:, 928:1696]], axis=1)
    w_gl = wl[:, 1696:]
    qb_w, kvb_w = b_w_qb[l], b_w_kvb[l]
    zq = lambda n: jnp.zeros((qb_w.shape[0], n), F32)
    zk = lambda n: jnp.zeros((kvb_w.shape[0], n), F32)
    per_q = B_NOPE + B_ROPE
    per_kv = B_NOPE + B_V
    qcols, kvcols = [], []
    for hd in range(B_HEADS):
        qcols += [qb_w[:, per_q * hd:per_q * (hd + 1)], zq(LANES - per_q)]
        kvcols += [kvb_w[:, per_kv * hd:per_kv * hd + B_NOPE], zk(LANES - B_NOPE),
                   kvb_w[:, per_kv * hd + B_NOPE:per_kv * (hd + 1)], zk(LANES - B_V)]
    wqb = jnp.concatenate(qcols, axis=1)
    wkvb = jnp.concatenate(kvcols, axis=1)
    wpool = jax.scipy.linalg.block_diag(*[d_w_pool[l, g] for g in range(POOL_GROUPS)])
    return w_small.astype(BF16), w_gl.astype(BF16), wqb.astype(BF16), wkvb.astype(BF16), wpool.astype(BF16)


def kernel(x, c, ctx, c_ctx, ada_w, ada_b, mix_pre_g, mix_post_g, ffn_pre_g, ffn_post_g, w_in, a_qn_g, a_kn_g,
           b_qa_g, b_kva_g, b_w_qb, b_w_kvb, c_sink, d_w_pool, d_scale, w_branch, w_out, w_ffn1, w_ffn3, w_ffn2):
    bsz, n_tok, d = x.shape
    depth = ada_w.shape[0]
    assert d == D_MODEL and ctx.shape[1] == CTX_LEN and n_tok % TM == 0 and bsz + 1 <= 16

    cs = jnp.zeros((16, d), F32).at[:bsz].set(c).at[bsz].set(c_ctx)
    mods = _mods_call(cs, ada_w, ada_b)
    tabs = _rope_tables(n_tok)
    pm, ph = _pool_matrices()
    xa = jnp.concatenate([ctx, x], axis=1)
    row2 = lambda v: v.reshape(1, -1)
    tile2 = lambda v: jnp.tile(v, 2).reshape(1, -1)

    for l in range(depth):
        last = l == depth - 1
        t0 = 1 if last else 0
        modt = jnp.stack([jnp.broadcast_to(mods[l, bsz], (bsz, 6 * d)), mods[l, :bsz]], axis=1)[:, :, None, :]
        w_small, w_gl, wqb, wkvb, wpool = _layer_weights(l, w_in, b_w_qb, b_w_kvb, d_w_pool)
        qa, kva, qb, kb, vb, qc, kvc, u = _pre_call(
            xa, modt, row2(mix_pre_g[l]), w_small, tile2(a_qn_g[l]), tile2(a_kn_g[l]),
            row2(b_qa_g[l]), row2(b_kva_g[l]), wqb, wkvb, tabs)
        oa = _dense_attn_call("attn_a", qa, [kva], _HEADS_A, not last)
        ob = _dense_attn_call("attn_b", qb, [kb, vb], _HEADS_B, not last)
        oc = _attn_call(_attn_c_kernel, "attn_c", qc, [kvc], t0, sink=c_sink[l])
        x1 = _merge_call(xa, modt, row2(mix_pre_g[l]), w_gl, oa, ob, oc, u, pm, ph, wpool, row2(d_scale[l]),
                         w_branch[l].reshape(N_BRANCH * BRANCH_WIDTH, d).astype(BF16), w_out[l].astype(BF16),
                         row2(mix_post_g[l]), t0)
        xa = _ffn_call(x1, modt, row2(ffn_pre_g[l]), w_ffn1[l].astype(BF16), w_ffn3[l].astype(BF16),
                       w_ffn2[l].astype(BF16), row2(ffn_post_g[l]), t0)
    return xa
```

```python
import functools

import numpy as np
import jax
import jax.numpy as jnp
from jax import lax
from jax.experimental import pallas as pl
from jax.experimental.pallas import tpu as pltpu

D_MODEL = 1024
CTX_LEN = 256
GRID_W = 64
HEAD_DIM = 64
ROPE_THETA = 10000.0
EPS = 1e-6
N_BRANCH = 4
BRANCH_WIDTH = 256
B_HEADS = 4
B_Q_RANK = 256
B_KV_RANK = 128
B_NOPE = 64
B_ROPE = 32
B_V = 64
WINDOW = 128
POOL_WINDOWS = (2, 4, 8, 16)
POOL_GROUPS = 4
POOL_DIM = 64
LOG2E = 1.4426950408889634

LANES = 128
TM = CTX_LEN
N_SUB = 2
RB = N_SUB * TM
HALO = 64
BAND = 2 * TM
N_SMALL = 1792
MOD_ROWS = 16
VMEM_LIMIT = 56 * 1024 * 1024

F32 = jnp.float32
BF16 = jnp.bfloat16


def _dot(a, b):
    return jnp.dot(a, b, preferred_element_type=F32)


def _dot_t(a, b):
    return lax.dot_general(a, b, (((1,), (1,)), ((), ())), preferred_element_type=F32)


def _rms(x, g):
    return x * lax.rsqrt(jnp.mean(x * x, axis=-1, keepdims=True) + EPS) * g


def _lane_lo(shape, width):
    lane = lax.broadcasted_iota(jnp.int32, shape, 1)
    return (lane % (2 * width)) < width


def _swap_halves(x, half):
    return jnp.where(_lane_lo(x.shape, half),
                     pltpu.roll(x, LANES - half, 1), pltpu.roll(x, half, 1))


def _rope(x, cos, sin, half):
    return x * cos + _swap_halves(x, half) * sin


def _head_rms(x, g):
    lo = _lane_lo(x.shape, HEAD_DIM)
    s = x * x
    s_lo = jnp.sum(jnp.where(lo, s, 0.0), axis=-1, keepdims=True)
    s_hi = jnp.sum(jnp.where(lo, 0.0, s), axis=-1, keepdims=True)
    ms = jnp.where(lo, s_lo, s_hi) * (1.0 / HEAD_DIM)
    return x * lax.rsqrt(ms + EPS) * g


def _pair_layouts(k, v):
    lo = _lane_lo(k.shape, HEAD_DIM)
    kr = pltpu.roll(k, HEAD_DIM, 1)
    vr = pltpu.roll(v, HEAD_DIM, 1)
    return (jnp.where(lo, k, kr), jnp.where(lo, v, 1.0), jnp.where(lo, kr, k), jnp.where(lo, vr, 1.0))


def _resident(shape):
    nd = len(shape)
    return pl.BlockSpec(shape, lambda *_: (0,) * nd, pipeline_mode=pl.Buffered(1))


def _params(n_axes=1):
    return pltpu.CompilerParams(dimension_semantics=("arbitrary",) * n_axes, vmem_limit_bytes=VMEM_LIMIT)


class _Rows:
    def __init__(self, bsz, n_tok):
        assert n_tok % RB == 0 and (bsz * CTX_LEN) % RB == 0
        self.bsz, self.n_tok = bsz, n_tok
        self.n_lat = bsz * n_tok
        self.n_all = self.n_lat + bsz * CTX_LEN
        self.per_b = n_tok // RB
        self.lat_blocks = self.n_lat // RB
        self.all_blocks = self.n_all // RB

    def rows(self, width, rows=RB, scale=1):
        return pl.BlockSpec((rows, width), lambda i: (i * scale, 0))

    def mod(self):
        return pl.BlockSpec((None, 1, 6 * D_MODEL),
                            lambda i: (jnp.where(i < self.lat_blocks, i // self.per_b, self.bsz), 0, 0))

    def table(self):
        return pl.BlockSpec((RB, LANES), lambda i: (jnp.where(i < self.lat_blocks, i % self.per_b, self.per_b), 0))


def _mods_kernel(c_ref, w_ref, b_ref, o_ref):
    c = c_ref[...]
    s = c / (1.0 + jnp.exp(-c))
    o_ref[...] = _dot(s.astype(BF16), w_ref[...].astype(BF16)) + b_ref[...]


def _mods_call(cs, ada_w, ada_b):
    depth, d, n = ada_w.shape
    nb = 1536
    return pl.pallas_call(
        _mods_kernel,
        grid=(depth, n // nb),
        in_specs=[pl.BlockSpec((MOD_ROWS, d), lambda l, j: (0, 0)),
                  pl.BlockSpec((None, d, nb), lambda l, j: (l, 0, j)),
                  pl.BlockSpec((None, 1, nb), lambda l, j: (l, 0, j))],
        out_specs=pl.BlockSpec((None, MOD_ROWS, nb), lambda l, j: (l, 0, j)),
        out_shape=jax.ShapeDtypeStruct((depth, MOD_ROWS, n), F32),
        compiler_params=_params(2),
        name="adaln_mods",
    )(cs, ada_w, ada_b.reshape(depth, 1, n))


def _pre_kernel(x_ref, mod_ref, g_ref, w_ref, gq_ref, gk_ref, gqa_ref, gkva_ref, wqb_ref, wkvb_ref,
                cosa_ref, sina_ref, cosb_ref, sinb_ref,
                qa_ref, kva_ref, qb_ref, kb_ref, vb_ref, qc_ref, kvc_ref, u_ref):
    d = D_MODEL
    sc_a = HEAD_DIM ** -0.5 * LOG2E
    sc_b = (B_NOPE + B_ROPE) ** -0.5 * LOG2E
    for sub in range(N_SUB):
        rows = slice(TM * sub, TM * (sub + 1))
        h = _rms(x_ref[rows, :], g_ref[...]) * (1.0 + mod_ref[:, d:2 * d]) + mod_ref[:, 0:d]
        z = _dot(h.astype(BF16), w_ref[...])
        cosa, sina = cosa_ref[rows, :], sina_ref[rows, :]
        cosb, sinb = cosb_ref[rows, :], sinb_ref[rows, :]

        for j in range(2):
            q = _head_rms(z[:, LANES * j:LANES * (j + 1)], gq_ref[...])
            qa_ref[rows, LANES * j:LANES * (j + 1)] = (_rope(q, cosa, sina, 32) * sc_a).astype(BF16)
        k = _rope(_head_rms(z[:, 256:384], gk_ref[...]), cosa, sina, 32)
        for i, blk in enumerate(_pair_layouts(k, z[:, 384:512])):
            kva_ref[rows, LANES * i:LANES * (i + 1)] = blk.astype(BF16)

        qn = _rms(z[:, 512:768], gqa_ref[...]).astype(BF16)
        qb = _dot(qn, wqb_ref[...]) * sc_b
        kvn = _rms(z[:, 768:896], gkva_ref[...]).astype(BF16)
        kvb = _dot(kvn, wkvb_ref[...])
        kbr = _rope(z[:, 896:1024], cosb, sinb, 16)
        lo = _lane_lo(kbr.shape, HEAD_DIM)
        for hd in range(B_HEADS):
            c0 = LANES * hd
            qb_ref[rows, c0:c0 + LANES] = _rope(qb[:, c0:c0 + LANES], cosb, sinb, 16).astype(BF16)
            kb_ref[rows, c0:c0 + LANES] = (kvb[:, 2 * c0:2 * c0 + LANES] + kbr).astype(BF16)
            vb_ref[rows, c0:c0 + LANES] = jnp.where(lo, kvb[:, 2 * c0 + LANES:2 * c0 + 2 * LANES], 1.0).astype(BF16)

        for j in range(2):
            q = z[:, 1024 + LANES * j:1024 + LANES * (j + 1)]
            qc_ref[rows, LANES * j:LANES * (j + 1)] = (_rope(q, cosa, sina, 32) * sc_a).astype(BF16)
        k = _rope(z[:, 1280:1408], cosa, sina, 32)
        for i, blk in enumerate(_pair_layouts(k, z[:, 1408:1536])):
            kvc_ref[rows, LANES * i:LANES * (i + 1)] = blk.astype(BF16)

        u_ref[rows, :] = z[:, 1536:1792]


def _pre_call(geo, xa, mods_l, g, w_small, gq, gk, gqa, gkva, wqb, wkvb, tabs):
    d = xa.shape[1]
    widths = (256, 512, 512, 512, 512, 256, 512)
    out_shape = [jax.ShapeDtypeStruct((geo.n_all, w), BF16) for w in widths]
    out_shape.append(jax.ShapeDtypeStruct((geo.n_all, 256), F32))
    return pl.pallas_call(
        _pre_kernel,
        grid=(geo.all_blocks,),
        in_specs=[geo.rows(d), geo.mod(), _resident((1, d)), _resident(w_small.shape),
                  _resident((1, LANES)), _resident((1, LANES)), _resident((1, B_Q_RANK)),
                  _resident((1, B_KV_RANK)), _resident(wqb.shape), _resident(wkvb.shape),
                  geo.table(), geo.table(), geo.table(), geo.table()],
        out_specs=[geo.rows(w) for w in widths] + [geo.rows(256)],
        out_shape=out_shape,
        compiler_params=_params(),
        name="pre_attn",
    )(xa, mods_l, g, w_small, gq, gk, gqa, gkva, wqb, wkvb, *tabs)


def _probs(scores, extra=None):
    m = functools.reduce(jnp.maximum, [jnp.max(s, axis=-1, keepdims=True) for s in scores])
    if extra is not None:
        m = jnp.maximum(m, extra)
    ps = [jnp.exp2(s - m).astype(BF16) for s in scores]
    return ps, (None if extra is None else jnp.exp2(extra - m))


def _normalise_pair(r0, r1, e0=None, e1=None):
    lo = _lane_lo(r0.shape, HEAD_DIM)
    r0r = pltpu.roll(r0, HEAD_DIM, 1)
    r1r = pltpu.roll(r1, HEAD_DIM, 1)
    den0 = r0r if e0 is None else r0r + e0
    den1 = r1 if e1 is None else r1 + e1
    return jnp.where(lo, r0 / den0, r1r / den1)


def _stack_pair(q):
    lo = _lane_lo(q.shape, HEAD_DIM)
    zero = jnp.zeros_like(q)
    return jnp.concatenate([jnp.where(lo, q, zero), jnp.where(lo, zero, q)], axis=0)


def _dense_attn_kernel(*refs, heads, n_kv, with_ctx):
    q_lat, q_ctx = refs[0:2]
    kv_lat = refs[2:2 + 2 * n_kv:2]
    kv_ctx = refs[3:3 + 2 * n_kv:2]
    o_lat, o_ctx = refs[2 + 2 * n_kv:4 + 2 * n_kv]
    scratch = refs[4 + 2 * n_kv:]
    s_bufs, p_bufs, r_bufs = scratch[0:2], scratch[2:4], scratch[4:6]
    n_tok = q_lat.shape[0]
    n_tiles = n_tok // TM
    n_heads = len(heads)

    def q_block(ref, row0, h):
        qcol, mask, *_ = heads[h]
        q = ref[pl.ds(row0, TM), LANES * qcol:LANES * (qcol + 1)]
        if mask is None:
            return q
        lo = _lane_lo(q.shape, HEAD_DIM)
        return jnp.where(lo if mask == "lo" else ~lo, q, jnp.zeros_like(q))

    def keys(refs_, h):
        _, _, karr, kcol, _, _ = heads[h]
        return refs_[karr][:, LANES * kcol:LANES * (kcol + 1)]

    def values(refs_, h):
        _, _, _, _, varr, vcol = heads[h]
        return refs_[varr][:, LANES * vcol:LANES * (vcol + 1)]

    def store_pair(ref, row0, j, r0, r1):
        ref[pl.ds(row0, TM), LANES * j:LANES * (j + 1)] = _normalise_pair(r0, r1).astype(BF16)

    if with_ctx:
        rs = []
        for h in range(n_heads):
            (p,), _ = _probs([_dot_t(q_block(q_ctx, 0, h), keys(kv_ctx, h))])
            rs.append(_dot(p, values(kv_ctx, h)))
        for j in range(n_heads // 2):
            store_pair(o_ctx, 0, j, rs[2 * j], rs[2 * j + 1])

    def scores(row0, h, buf):
        q = q_block(q_lat, row0, h)
        s_bufs[buf][:, 0:CTX_LEN] = _dot_t(q, keys(kv_ctx, h))
        s_bufs[buf][:, CTX_LEN:] = _dot_t(q, keys(kv_lat, h))

    def softmax(buf):
        (p,), _ = _probs([s_bufs[buf][...]])
        p_bufs[buf][...] = p

    def weighted(h, buf):
        return (_dot(p_bufs[buf][:, 0:CTX_LEN], values(kv_ctx, h))
                + _dot(p_bufs[buf][:, CTX_LEN:], values(kv_lat, h)))

    p_bufs[1][...] = jnp.zeros(p_bufs[1].shape, BF16)
    for rb in r_bufs:
        rb[...] = jnp.zeros(rb.shape, F32)
    scores(0, 0, 0)

    def tile_body(i, carry):
        row0 = pl.multiple_of(i * TM, TM)
        prev0 = pl.multiple_of(jnp.maximum(i - 1, 0) * TM, TM)
        next0 = pl.multiple_of(jnp.minimum(i + 1, n_tiles - 1) * TM, TM)
        for h in range(n_heads):
            buf = h % 2
            if h + 1 < n_heads:
                scores(row0, h + 1, 1 - buf)
            else:
                scores(next0, 0, 1 - buf)
            softmax(buf)
            hp = (h - 1) % n_heads
            r = weighted(hp, 1 - buf)
            if hp % 2 == 0:
                r_bufs[hp // 2][...] = r
            else:
                store_pair(o_lat, prev0 if h == 0 else row0, hp // 2, r_bufs[hp // 2][...], r)
        return carry

    lax.fori_loop(0, n_tiles, tile_body, 0)
    last = n_heads - 1
    store_pair(o_lat, n_tok - TM, last // 2, r_bufs[last // 2][...], weighted(last, last % 2))
    if not with_ctx:
        o_ctx[...] = jnp.zeros(o_ctx.shape, BF16)


_HEADS_A = tuple((h // 2, "lo" if h % 2 == 0 else "hi", 0, 2 * (h // 2), 0, 2 * (h // 2) + 1) for h in range(4))
_HEADS_B = tuple((h, None, 0, h, 1, h) for h in range(4))


def _batch_specs(geo, width):
    ctx0 = geo.n_lat // CTX_LEN
    return [pl.BlockSpec((geo.n_tok, width), lambda b: (b, 0)),
            pl.BlockSpec((CTX_LEN, width), lambda b: (ctx0 + b, 0))]


def _attn_out(geo):
    specs = [pl.BlockSpec((geo.n_tok, BRANCH_WIDTH), lambda b: (b, 0)),
             pl.BlockSpec((CTX_LEN, BRANCH_WIDTH), lambda b: (b, 0))]
    shapes = [jax.ShapeDtypeStruct((geo.n_lat, BRANCH_WIDTH), BF16),
              jax.ShapeDtypeStruct((geo.bsz * CTX_LEN, BRANCH_WIDTH), BF16)]
    return specs, shapes


def _dense_attn_call(name, geo, q, kvs, heads, with_ctx):
    s_len = geo.n_tok + CTX_LEN
    in_specs = _batch_specs(geo, q.shape[1])
    args = [q, q]
    for kv in kvs:
        in_specs += _batch_specs(geo, kv.shape[1])
        args += [kv, kv]
    out_specs, out_shape = _attn_out(geo)
    return pl.pallas_call(
        functools.partial(_dense_attn_kernel, heads=heads, n_kv=len(kvs), with_ctx=with_ctx),
        grid=(geo.bsz,),
        in_specs=in_specs,
        out_specs=out_specs,
        out_shape=out_shape,
        scratch_shapes=[pltpu.VMEM((TM, s_len), F32), pltpu.VMEM((TM, s_len), F32),
                        pltpu.VMEM((TM, s_len), BF16), pltpu.VMEM((TM, s_len), BF16),
                        pltpu.VMEM((TM, LANES), F32), pltpu.VMEM((TM, LANES), F32)],
        compiler_params=_params(),
        name=name,
    )(*args)


def _window_attn_kernel(sink_ref, q_lat, q_ctx, kv_lat, kv_ctx, o_lat, o_ctx, s0, s1, p0, p1, e0, e1, *, with_ctx):
    s_bufs, p_bufs, e_bufs = (s0, s1), (p0, p1), (e0, e1)
    n_tok = q_lat.shape[0]
    n_tiles = n_tok // TM
    n_keys = CTX_LEN + BAND

    def sink_col(j):
        row = lax.broadcasted_iota(jnp.int32, (2 * TM, 1), 0)
        return jnp.where(row < TM, sink_ref[2 * j], sink_ref[2 * j + 1]) * LOG2E

    def cols(j):
        return slice(2 * LANES * j, 2 * LANES * j + LANES), slice(2 * LANES * j + LANES, 2 * LANES * (j + 1))

    def store_pair(ref, row0, j, r, e):
        ref[pl.ds(row0, TM), LANES * j:LANES * (j + 1)] = _normalise_pair(r[:TM], r[TM:], e[:TM], e[TM:]).astype(BF16)

    if with_ctx:
        for j in range(2):
            c_kk, c_v1 = cols(j)
            (p,), e = _probs([_dot_t(_stack_pair(q_ctx[:, LANES * j:LANES * (j + 1)]), kv_ctx[:, c_kk])], sink_col(j))
            store_pair(o_ctx, 0, j, _dot(p, kv_ctx[:, c_v1]), e)

    def band_start(i):
        if isinstance(i, int):
            return min(max(i * TM - WINDOW, 0), n_tok - BAND)
        return pl.multiple_of(jnp.clip(i * TM - WINDOW, 0, n_tok - BAND), WINDOW)

    def scores(i, j, buf):
        c_kk, _ = cols(j)
        row0 = i * TM if isinstance(i, int) else pl.multiple_of(i * TM, TM)
        qs = _stack_pair(q_lat[pl.ds(row0, TM), LANES * j:LANES * (j + 1)])
        s_bufs[buf][:, 0:CTX_LEN] = _dot_t(qs, kv_ctx[:, c_kk])
        s_bufs[buf][:, CTX_LEN:] = _dot_t(qs, kv_lat[pl.ds(band_start(i), BAND), c_kk])

    def softmax(i, j, buf):
        row = lax.broadcasted_iota(jnp.int32, (2 * TM, BAND), 0)
        col = lax.broadcasted_iota(jnp.int32, (2 * TM, BAND), 1)
        rel = col - jnp.where(row < TM, row, row - TM) + (band_start(i) - i * TM)
        s_band = jnp.where(jnp.abs(rel) <= WINDOW, s_bufs[buf][:, CTX_LEN:], -1e30)
        (p_ctx, p_band), e = _probs([s_bufs[buf][:, 0:CTX_LEN], s_band], sink_col(j))
        p_bufs[buf][:, 0:CTX_LEN] = p_ctx
        p_bufs[buf][:, CTX_LEN:] = p_band
        e_bufs[buf][...] = jnp.broadcast_to(e, e_bufs[buf].shape)

    def weighted(i, j, buf):
        _, c_v1 = cols(j)
        return (_dot(p_bufs[buf][:, 0:CTX_LEN], kv_ctx[:, c_v1])
                + _dot(p_bufs[buf][:, CTX_LEN:], kv_lat[pl.ds(band_start(i), BAND), c_v1]))

    p_bufs[1][...] = jnp.zeros(p_bufs[1].shape, BF16)
    e_bufs[1][...] = jnp.ones(e_bufs[1].shape, F32)
    scores(0, 0, 0)

    def tile_body(i, carry):
        prev = jnp.maximum(i - 1, 0)
        nxt = jnp.minimum(i + 1, n_tiles - 1)
        scores(i, 1, 1)
        softmax(i, 0, 0)
        store_pair(o_lat, pl.multiple_of(prev * TM, TM), 1, weighted(prev, 1, 1), e_bufs[1][...])
        scores(nxt, 0, 0)
        softmax(i, 1, 1)
        store_pair(o_lat, pl.multiple_of(i * TM, TM), 0, weighted(i, 0, 0), e_bufs[0][...])
        return carry

    lax.fori_loop(0, n_tiles, tile_body, 0)
    store_pair(o_lat, n_tok - TM, 1, weighted(n_tiles - 1, 1, 1), e_bufs[1][...])
    if not with_ctx:
        o_ctx[...] = jnp.zeros(o_ctx.shape, BF16)


def _window_attn_call(geo, sink, q, kv, with_ctx):
    n_keys = CTX_LEN + BAND
    out_specs, out_shape = _attn_out(geo)
    return pl.pallas_call(
        functools.partial(_window_attn_kernel, with_ctx=with_ctx),
        grid=(geo.bsz,),
        in_specs=[pl.BlockSpec(memory_space=pltpu.SMEM)] + _batch_specs(geo, q.shape[1]) + _batch_specs(geo, kv.shape[1]),
        out_specs=out_specs,
        out_shape=out_shape,
        scratch_shapes=[pltpu.VMEM((2 * TM, n_keys), F32), pltpu.VMEM((2 * TM, n_keys), F32),
                        pltpu.VMEM((2 * TM, n_keys), BF16), pltpu.VMEM((2 * TM, n_keys), BF16),
                        pltpu.VMEM((2 * TM, LANES), F32), pltpu.VMEM((2 * TM, LANES), F32)],
        compiler_params=_params(),
        name="attn_c",
    )(sink, q, q, kv, kv)


def _split_bf16(x):
    hi = x.astype(BF16)
    return hi, (x - hi.astype(F32)).astype(BF16)


def _merge_kernel(x_ref, mod_ref, gpre_ref, wgl_ref, oal_ref, oac_ref, obl_ref, obc_ref, ocl_ref, occ_ref,
                  u_ref, up_ref, un_ref, pm_ref, ph_ref, wpool_ref, dscale_ref,
                  wbr_ref, wout_ref, gpost_ref, out_ref, *, lat_blocks, per_b):
    d = D_MODEL
    i = pl.program_id(0)
    is_ctx = i >= lat_blocks
    blk = i % per_b
    lane = lax.broadcasted_iota(jnp.int32, (TM, 2 * LANES), 1)
    grp = lane // POOL_DIM
    row = lax.broadcasted_iota(jnp.int32, (TM, 2 * LANES), 0)
    p_lo = jnp.left_shift(1, grp)
    n_tok = jnp.where(is_ctx, CTX_LEN, per_b * RB)
    zero_halo = jnp.zeros((HALO, 2 * LANES), F32)

    for sub in range(N_SUB):
        rows = slice(TM * sub, TM * (sub + 1))
        x = x_ref[rows, :]
        hb = (_rms(x, gpre_ref[...]) * (1.0 + mod_ref[:, d:2 * d]) + mod_ref[:, 0:d]).astype(BF16)

        u = u_ref[rows, :]
        if sub == 0:
            prev = jnp.where(jnp.logical_or(is_ctx, blk == 0), zero_halo, up_ref[...])
        else:
            prev = jnp.where(is_ctx, zero_halo, u_ref[TM * sub - HALO:TM * sub, :])
        if sub == N_SUB - 1:
            nxt = jnp.where(jnp.logical_or(is_ctx, blk == per_b - 1), zero_halo, un_ref[...])
        else:
            nxt = jnp.where(is_ctx, zero_halo, u_ref[TM * (sub + 1):TM * (sub + 1) + HALO, :])
        halo = jnp.concatenate([prev, nxt], axis=0)
        u_hi, u_lo = _split_bf16(u)
        h_hi, h_lo = _split_bf16(halo)
        sums = jnp.zeros((TM, 2 * LANES), F32)
        for gi in range(POOL_GROUPS):
            pm, ph = pm_ref[gi], ph_ref[gi]
            sg = _dot(pm, u_hi) + _dot(pm, u_lo) + _dot(ph, h_hi) + _dot(ph, h_lo)
            sums = jnp.where(grp == gi, sg, sums)
        pos = jnp.where(is_ctx, row, blk * RB + TM * sub + row)
        cnt = jnp.minimum(pos + p_lo, n_tok) - jnp.maximum(pos - p_lo, 0)
        diff = sums / cnt.astype(F32) - u
        o_d = _dot(diff.astype(BF16), wpool_ref[...]) * dscale_ref[...]

        branches = (jnp.where(is_ctx, oac_ref[rows, :], oal_ref[rows, :]),
                    jnp.where(is_ctx, obc_ref[rows, :], obl_ref[rows, :]),
                    jnp.where(is_ctx, occ_ref[rows, :], ocl_ref[rows, :]),
                    o_d.astype(BF16))
        acc = jnp.zeros((TM, d), F32)
        for k in range(N_BRANCH):
            gl = _dot(hb, wgl_ref[:, d * k:d * (k + 1)])
            proj = _dot(branches[k], wbr_ref[BRANCH_WIDTH * k:BRANCH_WIDTH * (k + 1), :])
            acc = acc + proj / (1.0 + jnp.exp(-gl))
        y = _dot(acc.astype(BF16), wout_ref[...])
        out_ref[rows, :] = x + mod_ref[:, 2 * d:3 * d] * _rms(y, gpost_ref[...])


def _merge_call(geo, n_blocks, xa, mods_l, gpre, wgl, o_pairs, u, pm, ph, wpool, dscale, wbr, wout, gpost):
    d = xa.shape[1]
    r = RB // HALO
    n_halo = geo.n_all // HALO
    up_spec = pl.BlockSpec((HALO, 256), lambda i: (jnp.maximum(i * r - 1, 0), 0))
    un_spec = pl.BlockSpec((HALO, 256), lambda i: (jnp.minimum((i + 1) * r, n_halo - 1), 0))
    ctx_last = geo.all_blocks - geo.lat_blocks - 1
    lat_spec = pl.BlockSpec((RB, 256), lambda i: (jnp.minimum(i, geo.lat_blocks - 1), 0))
    ctx_spec = pl.BlockSpec((RB, 256), lambda i: (jnp.clip(i - geo.lat_blocks, 0, ctx_last), 0))
    o_specs, o_args = [], []
    for o_l, o_c in o_pairs:
        o_specs += [lat_spec, ctx_spec]
        o_args += [o_l, o_c]
    return pl.pallas_call(
        functools.partial(_merge_kernel, lat_blocks=geo.lat_blocks, per_b=geo.per_b),
        grid=(n_blocks,),
        in_specs=[geo.rows(d), geo.mod(), _resident((1, d)), _resident(wgl.shape)] + o_specs
                 + [geo.rows(256), up_spec, un_spec, _resident(pm.shape), _resident(ph.shape),
                    _resident(wpool.shape), _resident((1, 256)),
                    _resident(wbr.shape), _resident(wout.shape), _resident((1, d))],
        out_specs=geo.rows(d),
        out_shape=jax.ShapeDtypeStruct((n_blocks * RB, d), F32),
        compiler_params=_params(),
        name="merge",
    )(xa, mods_l, gpre, wgl, *o_args, u, u, u, pm, ph, wpool, dscale, wbr, wout, gpost)


def _ffn_kernel(x_ref, mod_ref, gpre_ref, w1_ref, w3_ref, w2_ref, gpost_ref, out_ref):
    d = D_MODEL
    for sub in range(N_SUB):
        rows = slice(TM * sub, TM * (sub + 1))
        x = x_ref[rows, :]
        hb = (_rms(x, gpre_ref[...]) * (1.0 + mod_ref[:, 4 * d:5 * d]) + mod_ref[:, 3 * d:4 * d]).astype(BF16)
        a = _dot(hb, w1_ref[...])
        b = _dot(hb, w3_ref[...])
        act = (a / (1.0 + jnp.exp(-a)) * b).astype(BF16)
        f = _dot(act, w2_ref[...])
        out_ref[rows, :] = x + mod_ref[:, 5 * d:6 * d] * _rms(f, gpost_ref[...])


def _ffn_call(geo, x1, mods_l, gpre, w1, w3, w2, gpost):
    n, d = x1.shape
    return pl.pallas_call(
        _ffn_kernel,
        grid=(n // RB,),
        in_specs=[geo.rows(d), geo.mod(), _resident((1, d)), _resident(w1.shape),
                  _resident(w3.shape), _resident(w2.shape), _resident((1, d))],
        out_specs=geo.rows(d),
        out_shape=jax.ShapeDtypeStruct((n, d), F32),
        compiler_params=_params(),
        name="ffn",
    )(x1, mods_l, gpre, w1, w3, w2, gpost)


def _rope_tables(n_tok):
    rows = n_tok // GRID_W
    row = jnp.repeat(jnp.arange(rows), GRID_W).astype(F32)
    col = jnp.tile(jnp.arange(GRID_W), rows).astype(F32)

    def cos_sin(rot_dim):
        n_freq = rot_dim // 4
        inv = ROPE_THETA ** (-jnp.arange(n_freq, dtype=F32) / n_freq)
        ang = jnp.concatenate([row[:, None] * inv, col[:, None] * inv], axis=-1)
        return jnp.cos(ang), jnp.sin(ang)

    c64, s64 = cos_sin(HEAD_DIM)
    cos_a = jnp.tile(c64, (1, 4))
    sin_a = jnp.tile(jnp.concatenate([-s64, s64], axis=-1), (1, 2))
    c32, s32 = cos_sin(B_ROPE)
    ones = lambda n: jnp.ones((n_tok, n), F32)
    zeros = lambda n: jnp.zeros((n_tok, n), F32)
    pad = LANES - B_NOPE - B_ROPE
    cos_b = jnp.concatenate([ones(B_NOPE), c32, c32, ones(pad)], axis=-1)
    sin_b = jnp.concatenate([zeros(B_NOPE), -s32, s32, zeros(pad)], axis=-1)
    ident_c = jnp.ones((RB, LANES), F32)
    ident_s = jnp.zeros((RB, LANES), F32)
    return (jnp.concatenate([cos_a, ident_c]), jnp.concatenate([sin_a, ident_s]),
            jnp.concatenate([cos_b, ident_c]), jnp.concatenate([sin_b, ident_s]))


def _pool_matrices():
    i = np.arange(TM)[:, None]
    jm = np.arange(TM)[None, :]
    jh = np.arange(2 * HALO)[None, :]
    halo_pos = np.where(jh < HALO, jh - HALO, TM + jh - HALO)
    pm, ph = [], []
    for win in POOL_WINDOWS:
        lo, hi = win // 2, win - win // 2 - 1
        pm.append(((jm - i >= -lo) & (jm - i <= hi)).astype(np.float32))
        ph.append(((halo_pos - i >= -lo) & (halo_pos - i <= hi)).astype(np.float32))
    return jnp.asarray(np.stack(pm), BF16), jnp.asarray(np.stack(ph), BF16)


def _layer_weights(l, w_in, b_w_qb, b_w_kvb, d_w_pool):
    wl = w_in[l]
    d = wl.shape[0]
    zeros = lambda n: jnp.zeros((d, n), F32)
    w_small = jnp.concatenate([wl[:, 0:896], zeros(B_NOPE), wl[:, 896:928], zeros(LANES - B_NOPE - B_ROPE),
                               wl[:, 928:1696]], axis=1)
    w_gl = wl[:, 1696:]
    qb_w, kvb_w = b_w_qb[l], b_w_kvb[l]
    zq = lambda n: jnp.zeros((qb_w.shape[0], n), F32)
    zk = lambda n: jnp.zeros((kvb_w.shape[0], n), F32)
    per_q = B_NOPE + B_ROPE
    per_kv = B_NOPE + B_V
    qcols, kvcols = [], []
    for hd in range(B_HEADS):
        qcols += [qb_w[:, per_q * hd:per_q * (hd + 1)], zq(LANES - per_q)]
        kvcols += [kvb_w[:, per_kv * hd:per_kv * hd + B_NOPE], zk(LANES - B_NOPE),
                   kvb_w[:, per_kv * hd + B_NOPE:per_kv * (hd + 1)], zk(LANES - B_V)]
    wqb = jnp.concatenate(qcols, axis=1)
    wkvb = jnp.concatenate(kvcols, axis=1)
    wpool = jax.scipy.linalg.block_diag(*[d_w_pool[l, g] for g in range(POOL_GROUPS)])
    return w_small.astype(BF16), w_gl.astype(BF16), wqb.astype(BF16), wkvb.astype(BF16), wpool.astype(BF16)


def kernel(x, c, ctx, c_ctx, ada_w, ada_b, mix_pre_g, mix_post_g, ffn_pre_g, ffn_post_g, w_in, a_qn_g, a_kn_g,
           b_qa_g, b_kva_g, b_w_qb, b_w_kvb, c_sink, d_w_pool, d_scale, w_branch, w_out, w_ffn1, w_ffn3, w_ffn2):
    bsz, n_tok, d = x.shape
    depth = ada_w.shape[0]
    assert d == D_MODEL and ctx.shape[1] == CTX_LEN and bsz + 1 <= MOD_ROWS
    geo = _Rows(bsz, n_tok)

    cs = jnp.zeros((MOD_ROWS, d), F32).at[:bsz].set(c).at[bsz].set(c_ctx)
    mods = _mods_call(cs, ada_w, ada_b)[:, :, None, :]
    tabs = _rope_tables(n_tok)
    pm, ph = _pool_matrices()
    xa = jnp.concatenate([x.reshape(geo.n_lat, d), ctx.reshape(bsz * CTX_LEN, d)], axis=0)
    row2 = lambda v: v.reshape(1, -1)
    tile2 = lambda v: jnp.tile(v, 2).reshape(1, -1)

    for l in range(depth):
        with_ctx = l < depth - 1
        n_blocks = geo.all_blocks if with_ctx else geo.lat_blocks
        w_small, w_gl, wqb, wkvb, wpool = _layer_weights(l, w_in, b_w_qb, b_w_kvb, d_w_pool)
        qa, kva, qb, kb, vb, qc, kvc, u = _pre_call(
            geo, xa, mods[l], row2(mix_pre_g[l]), w_small, tile2(a_qn_g[l]), tile2(a_kn_g[l]),
            row2(b_qa_g[l]), row2(b_kva_g[l]), wqb, wkvb, tabs)
        oa = _dense_attn_call("attn_a", geo, qa, [kva], _HEADS_A, with_ctx)
        ob = _dense_attn_call("attn_b", geo, qb, [kb, vb], _HEADS_B, with_ctx)
        oc = _window_attn_call(geo, c_sink[l], qc, kvc, with_ctx)
        x1 = _merge_call(geo, n_blocks, xa, mods[l], row2(mix_pre_g[l]), w_gl, [oa, ob, oc], u, pm, ph, wpool,
                         row2(d_scale[l]), w_branch[l].reshape(N_BRANCH * BRANCH_WIDTH, d).astype(BF16),
                         w_out[l].astype(BF16), row2(mix_post_g[l]))
        xa = _ffn_call(geo, x1, mods[l], row2(ffn_pre_g[l]), w_ffn1[l].astype(BF16), w_ffn3[l].astype(BF16),
                       w_ffn2[l].astype(BF16), row2(ffn_post_g[l]))
    return xa.reshape(bsz, n_tok, d)
```

```python
import functools

import numpy as np
import jax
import jax.numpy as jnp
from jax import lax
from jax.experimental import pallas as pl
from jax.experimental.pallas import tpu as pltpu

D_MODEL = 1024
CTX_LEN = 256
GRID_W = 64
HEAD_DIM = 64
ROPE_THETA = 10000.0
EPS = 1e-6
N_BRANCH = 4
BRANCH_WIDTH = 256
B_HEADS = 4
B_Q_RANK = 256
B_KV_RANK = 128
B_NOPE = 64
B_ROPE = 32
B_V = 64
WINDOW = 128
POOL_WINDOWS = (2, 4, 8, 16)
POOL_GROUPS = 4
POOL_DIM = 64
LOG2E = 1.4426950408889634

LANES = 128
TM = CTX_LEN
N_SUB = 2
RB = N_SUB * TM
HALO = 64
BAND = 2 * TM
N_SMALL = 1792
MOD_ROWS = 16
VMEM_LIMIT = 56 * 1024 * 1024

F32 = jnp.float32
BF16 = jnp.bfloat16


def _dot(a, b):
    return jnp.dot(a, b, preferred_element_type=F32)


def _dot_t(a, b):
    return lax.dot_general(a, b, (((1,), (1,)), ((), ())), preferred_element_type=F32)


def _rms(x, g):
    return x * lax.rsqrt(jnp.mean(x * x, axis=-1, keepdims=True) + EPS) * g


def _lane_lo(shape, width):
    lane = lax.broadcasted_iota(jnp.int32, shape, 1)
    return (lane % (2 * width)) < width


def _head0(shape):
    return _lane_lo(shape, HEAD_DIM // 2)


def _rope(x, cos, sin):
    return x * cos + pltpu.roll(x, LANES // 2, 1) * sin


def _head_rms(x, g):
    h0 = _head0(x.shape)
    s = x * x
    s0 = jnp.sum(jnp.where(h0, s, 0.0), axis=-1, keepdims=True)
    s1 = jnp.sum(jnp.where(h0, 0.0, s), axis=-1, keepdims=True)
    ms = jnp.where(h0, s0, s1) * (1.0 / HEAD_DIM)
    return x * lax.rsqrt(ms + EPS) * g


def _pair_layouts(k, v):
    h0 = _head0(k.shape)
    lo = _lane_lo(v.shape, HEAD_DIM)
    k_up = pltpu.roll(k, HEAD_DIM // 2, 1)
    k_dn = pltpu.roll(k, LANES - HEAD_DIM // 2, 1)
    vr = pltpu.roll(v, HEAD_DIM, 1)
    return (jnp.where(h0, k, k_up), jnp.where(lo, v, 1.0), jnp.where(h0, k_dn, k), jnp.where(lo, vr, 1.0))


def _resident(arr, layer=None):
    if layer is None:
        nd = arr.ndim
        return pl.BlockSpec(arr.shape, lambda *_: (0,) * nd, pipeline_mode=pl.Buffered(1))
    nd = arr.ndim - 1
    return pl.BlockSpec((None,) + arr.shape[1:], lambda *_: (layer,) + (0,) * nd, pipeline_mode=pl.Buffered(1))


def _params(n_axes=1):
    return pltpu.CompilerParams(dimension_semantics=("arbitrary",) * n_axes, vmem_limit_bytes=VMEM_LIMIT)


class _Rows:
    def __init__(self, bsz, n_tok):
        assert n_tok % RB == 0 and (bsz * CTX_LEN) % RB == 0
        self.bsz, self.n_tok = bsz, n_tok
        self.n_lat = bsz * n_tok
        self.n_all = self.n_lat + bsz * CTX_LEN
        self.per_b = n_tok // RB
        self.lat_blocks = self.n_lat // RB
        self.all_blocks = self.n_all // RB

    def rows(self, width):
        return pl.BlockSpec((RB, width), lambda i: (i, 0))

    def lat_src(self, arr):
        last = self.lat_blocks - 1
        if arr.ndim == 3:
            return pl.BlockSpec((None, RB, arr.shape[2]),
                                lambda i: (jnp.minimum(i, last) // self.per_b, jnp.minimum(i, last) % self.per_b, 0))
        return pl.BlockSpec((RB, arr.shape[1]), lambda i: (jnp.minimum(i, last), 0))

    def ctx_src(self, arr):
        ctx_last = self.all_blocks - self.lat_blocks - 1
        rel = lambda i: jnp.clip(i - self.lat_blocks, 0, ctx_last)
        if arr.ndim == 3:
            return pl.BlockSpec((N_SUB, CTX_LEN, arr.shape[2]), lambda i: (rel(i), 0, 0))
        base = (arr.shape[0] - self.bsz * CTX_LEN) // RB
        return pl.BlockSpec((RB, arr.shape[1]), lambda i: (base + rel(i), 0))

    def mod(self, layer):
        return pl.BlockSpec((None, None, 1, 6 * D_MODEL),
                            lambda i: (layer, jnp.where(i < self.lat_blocks, i // self.per_b, self.bsz), 0, 0))

    def table(self):
        return pl.BlockSpec((RB, LANES), lambda i: (jnp.where(i < self.lat_blocks, i % self.per_b, self.per_b), 0))


def _mods_kernel(c_ref, w_ref, b_ref, o_ref):
    c = c_ref[...]
    s = c / (1.0 + jnp.exp(-c))
    o_ref[...] = _dot(s.astype(BF16), w_ref[...].astype(BF16)) + b_ref[...]


def _mods_call(cs, ada_w, ada_b):
    depth, d, n = ada_w.shape
    nb = 1536
    return pl.pallas_call(
        _mods_kernel,
        grid=(depth, n // nb),
        in_specs=[pl.BlockSpec((MOD_ROWS, d), lambda l, j: (0, 0)),
                  pl.BlockSpec((None, d, nb), lambda l, j: (l, 0, j)),
                  pl.BlockSpec((None, 1, nb), lambda l, j: (l, 0, j))],
        out_specs=pl.BlockSpec((None, MOD_ROWS, nb), lambda l, j: (l, 0, j)),
        out_shape=jax.ShapeDtypeStruct((depth, MOD_ROWS, n), F32),
        compiler_params=_params(2),
        name="adaln_mods",
    )(cs, ada_w, ada_b.reshape(depth, 1, n))


def _sub_rows(ref, sub):
    return ref[sub] if len(ref.shape) == 3 else ref[TM * sub:TM * (sub + 1), :]


def _pre_kernel(xl_ref, xc_ref, mod_ref, g_ref, w_ref, gq_ref, gk_ref, gqa_ref, gkva_ref, wqb_ref, wkvb_ref,
                cosa_ref, sina_ref, cosb_ref, sinb_ref,
                qa_ref, kva_ref, qb_ref, kb_ref, vb_ref, qc_ref, kvc_ref, u_ref, *, lat_blocks):
    d = D_MODEL
    is_ctx = pl.program_id(0) >= lat_blocks
    sc_a = HEAD_DIM ** -0.5 * LOG2E
    sc_b = (B_NOPE + B_ROPE) ** -0.5 * LOG2E
    for sub in range(N_SUB):
        rows = slice(TM * sub, TM * (sub + 1))
        x = jnp.where(is_ctx, _sub_rows(xc_ref, sub), _sub_rows(xl_ref, sub))
        h = _rms(x, g_ref[...]) * (1.0 + mod_ref[:, d:2 * d]) + mod_ref[:, 0:d]
        z = _dot(h.astype(BF16), w_ref[...])
        cosa, sina = cosa_ref[rows, :], sina_ref[rows, :]
        cosb, sinb = cosb_ref[rows, :], sinb_ref[rows, :]

        for j in range(2):
            q = _head_rms(z[:, LANES * j:LANES * (j + 1)], gq_ref[...])
            qa_ref[rows, LANES * j:LANES * (j + 1)] = (_rope(q, cosa, sina) * sc_a).astype(BF16)
        k = _rope(_head_rms(z[:, 256:384], gk_ref[...]), cosa, sina)
        for i, blk in enumerate(_pair_layouts(k, z[:, 384:512])):
            kva_ref[rows, LANES * i:LANES * (i + 1)] = blk.astype(BF16)

        qn = _rms(z[:, 512:768], gqa_ref[...]).astype(BF16)
        qb = _dot(qn, wqb_ref[...]) * sc_b
        kvn = _rms(z[:, 768:896], gkva_ref[...]).astype(BF16)
        kvb = _dot(kvn, wkvb_ref[...])
        kbr = _rope(z[:, 896:1024], cosb, sinb)
        lo = _lane_lo(kbr.shape, HEAD_DIM)
        for hd in range(B_HEADS):
            c0 = LANES * hd
            qb_ref[rows, c0:c0 + LANES] = _rope(qb[:, c0:c0 + LANES], cosb, sinb).astype(BF16)
            kb_ref[rows, c0:c0 + LANES] = (kvb[:, 2 * c0:2 * c0 + LANES] + kbr).astype(BF16)
            vb_ref[rows, c0:c0 + LANES] = jnp.where(lo, kvb[:, 2 * c0 + LANES:2 * c0 + 2 * LANES], 1.0).astype(BF16)

        for j in range(2):
            q = z[:, 1024 + LANES * j:1024 + LANES * (j + 1)]
            qc_ref[rows, LANES * j:LANES * (j + 1)] = (_rope(q, cosa, sina) * sc_a).astype(BF16)
        k = _rope(z[:, 1280:1408], cosa, sina)
        for i, blk in enumerate(_pair_layouts(k, z[:, 1408:1536])):
            kvc_ref[rows, LANES * i:LANES * (i + 1)] = blk.astype(BF16)

        u_ref[rows, :] = z[:, 1536:1792]


def _pre_call(geo, l, x_lat, x_ctx, mods, wts, tabs):
    widths = (256, 512, 512, 512, 512, 256, 512)
    out_shape = [jax.ShapeDtypeStruct((geo.n_all, w), BF16) for w in widths]
    out_shape.append(jax.ShapeDtypeStruct((geo.n_all, 256), F32))
    consts = [wts[k] for k in ("mix_pre_g", "w_small", "a_qn_g", "a_kn_g", "b_qa_g", "b_kva_g", "wqb", "wkvb")]
    return pl.pallas_call(
        functools.partial(_pre_kernel, lat_blocks=geo.lat_blocks),
        grid=(geo.all_blocks,),
        in_specs=[geo.lat_src(x_lat), geo.ctx_src(x_ctx), geo.mod(l)] + [_resident(a, l) for a in consts]
                 + [geo.table()] * 4,
        out_specs=[geo.rows(w) for w in widths] + [geo.rows(256)],
        out_shape=out_shape,
        compiler_params=_params(),
        name="pre_attn",
    )(x_lat, x_ctx, mods, *consts, *tabs)


def _probs(scores, extra=None):
    m = functools.reduce(jnp.maximum, [jnp.max(s, axis=-1, keepdims=True) for s in scores])
    if extra is not None:
        m = jnp.maximum(m, extra)
    ps = [jnp.exp2(s - m).astype(BF16) for s in scores]
    return ps, (None if extra is None else jnp.exp2(extra - m))


def _normalise_pair(r0, r1, e0=None, e1=None):
    lo = _lane_lo(r0.shape, HEAD_DIM)
    r0r = pltpu.roll(r0, HEAD_DIM, 1)
    r1r = pltpu.roll(r1, HEAD_DIM, 1)
    den0 = r0r if e0 is None else r0r + e0
    den1 = r1 if e1 is None else r1 + e1
    return jnp.where(lo, r0 / den0, r1r / den1)


def _stack_pair(q):
    h0 = _head0(q.shape)
    zero = jnp.zeros_like(q)
    return jnp.concatenate([jnp.where(h0, q, zero), jnp.where(h0, zero, q)], axis=0)


def _dense_attn_kernel(*refs, heads, n_kv, with_ctx):
    q_lat, q_ctx = refs[0:2]
    kv_lat = refs[2:2 + 2 * n_kv:2]
    kv_ctx = refs[3:3 + 2 * n_kv:2]
    o_lat, o_ctx = refs[2 + 2 * n_kv:4 + 2 * n_kv]
    scratch = refs[4 + 2 * n_kv:]
    s_bufs, p_bufs, r_bufs = scratch[0:2], scratch[2:4], scratch[4:6]
    n_tok = q_lat.shape[0]
    n_tiles = n_tok // TM
    n_heads = len(heads)

    def q_block(ref, row0, h):
        qcol, mask, *_ = heads[h]
        q = ref[pl.ds(row0, TM), LANES * qcol:LANES * (qcol + 1)]
        if mask is None:
            return q
        h0 = _head0(q.shape)
        return jnp.where(h0 if mask == "head0" else ~h0, q, jnp.zeros_like(q))

    def keys(refs_, h):
        _, _, karr, kcol, _, _ = heads[h]
        return refs_[karr][:, LANES * kcol:LANES * (kcol + 1)]

    def values(refs_, h):
        _, _, _, _, varr, vcol = heads[h]
        return refs_[varr][:, LANES * vcol:LANES * (vcol + 1)]

    def store_pair(ref, row0, j, r0, r1):
        ref[pl.ds(row0, TM), LANES * j:LANES * (j + 1)] = _normalise_pair(r0, r1).astype(BF16)

    if with_ctx:
        rs = []
        for h in range(n_heads):
            (p,), _ = _probs([_dot_t(q_block(q_ctx, 0, h), keys(kv_ctx, h))])
            rs.append(_dot(p, values(kv_ctx, h)))
        for j in range(n_heads // 2):
            store_pair(o_ctx, 0, j, rs[2 * j], rs[2 * j + 1])

    def scores(row0, h, buf):
        q = q_block(q_lat, row0, h)
        s_bufs[buf][:, 0:CTX_LEN] = _dot_t(q, keys(kv_ctx, h))
        s_bufs[buf][:, CTX_LEN:] = _dot_t(q, keys(kv_lat, h))

    def softmax(buf):
        (p,), _ = _probs([s_bufs[buf][...]])
        p_bufs[buf][...] = p

    def weighted(h, buf):
        return (_dot(p_bufs[buf][:, 0:CTX_LEN], values(kv_ctx, h))
                + _dot(p_bufs[buf][:, CTX_LEN:], values(kv_lat, h)))

    p_bufs[1][...] = jnp.zeros(p_bufs[1].shape, BF16)
    for rb in r_bufs:
        rb[...] = jnp.zeros(rb.shape, F32)
    scores(0, 0, 0)

    def tile_body(i, carry):
        row0 = pl.multiple_of(i * TM, TM)
        prev0 = pl.multiple_of(jnp.maximum(i - 1, 0) * TM, TM)
        next0 = pl.multiple_of(jnp.minimum(i + 1, n_tiles - 1) * TM, TM)
        for h in range(n_heads):
            buf = h % 2
            if h + 1 < n_heads:
                scores(row0, h + 1, 1 - buf)
            else:
                scores(next0, 0, 1 - buf)
            softmax(buf)
            hp = (h - 1) % n_heads
            r = weighted(hp, 1 - buf)
            if hp % 2 == 0:
                r_bufs[hp // 2][...] = r
            else:
                store_pair(o_lat, prev0 if h == 0 else row0, hp // 2, r_bufs[hp // 2][...], r)
        return carry

    lax.fori_loop(0, n_tiles, tile_body, 0)
    last = n_heads - 1
    store_pair(o_lat, n_tok - TM, last // 2, r_bufs[last // 2][...], weighted(last, last % 2))
    if not with_ctx:
        o_ctx[...] = jnp.zeros(o_ctx.shape, BF16)


_HEADS_A = tuple((h // 2, "head0" if h % 2 == 0 else "head1", 0, 2 * (h // 2), 0, 2 * (h // 2) + 1)
                 for h in range(4))
_HEADS_B = tuple((h, None, 0, h, 1, h) for h in range(4))


def _batch_specs(geo, width):
    ctx0 = geo.n_lat // CTX_LEN
    return [pl.BlockSpec((geo.n_tok, width), lambda b: (b, 0)),
            pl.BlockSpec((CTX_LEN, width), lambda b: (ctx0 + b, 0))]


def _attn_out(geo):
    specs = [pl.BlockSpec((geo.n_tok, BRANCH_WIDTH), lambda b: (b, 0)),
             pl.BlockSpec((CTX_LEN, BRANCH_WIDTH), lambda b: (b, 0))]
    shapes = [jax.ShapeDtypeStruct((geo.n_lat, BRANCH_WIDTH), BF16),
              jax.ShapeDtypeStruct((geo.bsz * CTX_LEN, BRANCH_WIDTH), BF16)]
    return specs, shapes


def _dense_attn_call(name, geo, q, kvs, heads, with_ctx):
    s_len = geo.n_tok + CTX_LEN
    in_specs = _batch_specs(geo, q.shape[1])
    args = [q, q]
    for kv in kvs:
        in_specs += _batch_specs(geo, kv.shape[1])
        args += [kv, kv]
    out_specs, out_shape = _attn_out(geo)
    return pl.pallas_call(
        functools.partial(_dense_attn_kernel, heads=heads, n_kv=len(kvs), with_ctx=with_ctx),
        grid=(geo.bsz,),
        in_specs=in_specs,
        out_specs=out_specs,
        out_shape=out_shape,
        scratch_shapes=[pltpu.VMEM((TM, s_len), F32), pltpu.VMEM((TM, s_len), F32),
                        pltpu.VMEM((TM, s_len), BF16), pltpu.VMEM((TM, s_len), BF16),
                        pltpu.VMEM((TM, LANES), F32), pltpu.VMEM((TM, LANES), F32)],
        compiler_params=_params(),
        name=name,
    )(*args)


def _window_attn_kernel(sink_ref, q_lat, q_ctx, kv_lat, kv_ctx, o_lat, o_ctx, s0, s1, p0, p1, e0, e1, *,
                        with_ctx, layer):
    s_bufs, p_bufs, e_bufs = (s0, s1), (p0, p1), (e0, e1)
    n_tok = q_lat.shape[0]
    n_tiles = n_tok // TM
    n_keys = CTX_LEN + BAND

    def sink_col(j):
        row = lax.broadcasted_iota(jnp.int32, (2 * TM, 1), 0)
        return jnp.where(row < TM, sink_ref[layer, 2 * j], sink_ref[layer, 2 * j + 1]) * LOG2E

    def cols(j):
        return slice(2 * LANES * j, 2 * LANES * j + LANES), slice(2 * LANES * j + LANES, 2 * LANES * (j + 1))

    def store_pair(ref, row0, j, r, e):
        ref[pl.ds(row0, TM), LANES * j:LANES * (j + 1)] = _normalise_pair(r[:TM], r[TM:], e[:TM], e[TM:]).astype(BF16)

    if with_ctx:
        for j in range(2):
            c_kk, c_v1 = cols(j)
            (p,), e = _probs([_dot_t(_stack_pair(q_ctx[:, LANES * j:LANES * (j + 1)]), kv_ctx[:, c_kk])], sink_col(j))
            store_pair(o_ctx, 0, j, _dot(p, kv_ctx[:, c_v1]), e)

    def band_start(i):
        if isinstance(i, int):
            return min(max(i * TM - WINDOW, 0), n_tok - BAND)
        return pl.multiple_of(jnp.clip(i * TM - WINDOW, 0, n_tok - BAND), WINDOW)

    def scores(i, j, buf):
        c_kk, _ = cols(j)
        row0 = i * TM if isinstance(i, int) else pl.multiple_of(i * TM, TM)
        qs = _stack_pair(q_lat[pl.ds(row0, TM), LANES * j:LANES * (j + 1)])
        s_bufs[buf][:, 0:CTX_LEN] = _dot_t(qs, kv_ctx[:, c_kk])
        s_bufs[buf][:, CTX_LEN:] = _dot_t(qs, kv_lat[pl.ds(band_start(i), BAND), c_kk])

    def softmax(i, j, buf):
        row = lax.broadcasted_iota(jnp.int32, (2 * TM, BAND), 0)
        col = lax.broadcasted_iota(jnp.int32, (2 * TM, BAND), 1)
        rel = col - jnp.where(row < TM, row, row - TM) + (band_start(i) - i * TM)
        s_band = jnp.where(jnp.abs(rel) <= WINDOW, s_bufs[buf][:, CTX_LEN:], -1e30)
        (p_ctx, p_band), e = _probs([s_bufs[buf][:, 0:CTX_LEN], s_band], sink_col(j))
        p_bufs[buf][:, 0:CTX_LEN] = p_ctx
        p_bufs[buf][:, CTX_LEN:] = p_band
        e_bufs[buf][...] = jnp.broadcast_to(e, e_bufs[buf].shape)

    def weighted(i, j, buf):
        _, c_v1 = cols(j)
        return (_dot(p_bufs[buf][:, 0:CTX_LEN], kv_ctx[:, c_v1])
                + _dot(p_bufs[buf][:, CTX_LEN:], kv_lat[pl.ds(band_start(i), BAND), c_v1]))

    p_bufs[1][...] = jnp.zeros(p_bufs[1].shape, BF16)
    e_bufs[1][...] = jnp.ones(e_bufs[1].shape, F32)
    scores(0, 0, 0)

    def tile_body(i, carry):
        prev = jnp.maximum(i - 1, 0)
        nxt = jnp.minimum(i + 1, n_tiles - 1)
        scores(i, 1, 1)
        softmax(i, 0, 0)
        store_pair(o_lat, pl.multiple_of(prev * TM, TM), 1, weighted(prev, 1, 1), e_bufs[1][...])
        scores(nxt, 0, 0)
        softmax(i, 1, 1)
        store_pair(o_lat, pl.multiple_of(i * TM, TM), 0, weighted(i, 0, 0), e_bufs[0][...])
        return carry

    lax.fori_loop(0, n_tiles, tile_body, 0)
    store_pair(o_lat, n_tok - TM, 1, weighted(n_tiles - 1, 1, 1), e_bufs[1][...])
    if not with_ctx:
        o_ctx[...] = jnp.zeros(o_ctx.shape, BF16)


def _window_attn_call(geo, l, sink, q, kv, with_ctx):
    n_keys = CTX_LEN + BAND
    out_specs, out_shape = _attn_out(geo)
    return pl.pallas_call(
        functools.partial(_window_attn_kernel, with_ctx=with_ctx, layer=l),
        grid=(geo.bsz,),
        in_specs=[pl.BlockSpec(memory_space=pltpu.SMEM)] + _batch_specs(geo, q.shape[1]) + _batch_specs(geo, kv.shape[1]),
        out_specs=out_specs,
        out_shape=out_shape,
        scratch_shapes=[pltpu.VMEM((2 * TM, n_keys), F32), pltpu.VMEM((2 * TM, n_keys), F32),
                        pltpu.VMEM((2 * TM, n_keys), BF16), pltpu.VMEM((2 * TM, n_keys), BF16),
                        pltpu.VMEM((2 * TM, LANES), F32), pltpu.VMEM((2 * TM, LANES), F32)],
        compiler_params=_params(),
        name="attn_c",
    )(sink, q, q, kv, kv)


def _split_bf16(x):
    hi = x.astype(BF16)
    return hi, (x - hi.astype(F32)).astype(BF16)


def _merge_kernel(xl_ref, xc_ref, mod_ref, oal_ref, oac_ref, obl_ref, obc_ref, ocl_ref, occ_ref,
                  u_ref, up_ref, un_ref, pm_ref, ph_ref, gpre_ref, wgl_ref, wpool_ref, dscale_ref,
                  wbr_ref, wout_ref, gpost_ref, out_ref, *, lat_blocks, per_b):
    d = D_MODEL
    i = pl.program_id(0)
    is_ctx = i >= lat_blocks
    blk = i % per_b
    lane = lax.broadcasted_iota(jnp.int32, (TM, 2 * LANES), 1)
    grp = lane // POOL_DIM
    row = lax.broadcasted_iota(jnp.int32, (TM, 2 * LANES), 0)
    p_lo = jnp.left_shift(1, grp)
    n_tok = jnp.where(is_ctx, CTX_LEN, per_b * RB)
    zero_halo = jnp.zeros((HALO, 2 * LANES), F32)

    for sub in range(N_SUB):
        rows = slice(TM * sub, TM * (sub + 1))
        x = jnp.where(is_ctx, _sub_rows(xc_ref, sub), _sub_rows(xl_ref, sub))
        hb = (_rms(x, gpre_ref[...]) * (1.0 + mod_ref[:, d:2 * d]) + mod_ref[:, 0:d]).astype(BF16)

        u = u_ref[rows, :]
        if sub == 0:
            prev = jnp.where(jnp.logical_or(is_ctx, blk == 0), zero_halo, up_ref[...])
        else:
            prev = jnp.where(is_ctx, zero_halo, u_ref[TM * sub - HALO:TM * sub, :])
        if sub == N_SUB - 1:
            nxt = jnp.where(jnp.logical_or(is_ctx, blk == per_b - 1), zero_halo, un_ref[...])
        else:
            nxt = jnp.where(is_ctx, zero_halo, u_ref[TM * (sub + 1):TM * (sub + 1) + HALO, :])
        halo = jnp.concatenate([prev, nxt], axis=0)
        u_hi, u_lo = _split_bf16(u)
        h_hi, h_lo = _split_bf16(halo)
        sums = jnp.zeros((TM, 2 * LANES), F32)
        for gi in range(POOL_GROUPS):
            pm, ph = pm_ref[gi], ph_ref[gi]
            sg = _dot(pm, u_hi) + _dot(pm, u_lo) + _dot(ph, h_hi) + _dot(ph, h_lo)
            sums = jnp.where(grp == gi, sg, sums)
        pos = jnp.where(is_ctx, row, blk * RB + TM * sub + row)
        cnt = jnp.minimum(pos + p_lo, n_tok) - jnp.maximum(pos - p_lo, 0)
        diff = sums / cnt.astype(F32) - u
        o_d = _dot(diff.astype(BF16), wpool_ref[...]) * dscale_ref[...]

        branches = (jnp.where(is_ctx, oac_ref[rows, :], oal_ref[rows, :]),
                    jnp.where(is_ctx, obc_ref[rows, :], obl_ref[rows, :]),
                    jnp.where(is_ctx, occ_ref[rows, :], ocl_ref[rows, :]),
                    o_d.astype(BF16))
        acc = jnp.zeros((TM, d), F32)
        for k in range(N_BRANCH):
            gl = _dot(hb, wgl_ref[:, d * k:d * (k + 1)])
            proj = _dot(branches[k], wbr_ref[BRANCH_WIDTH * k:BRANCH_WIDTH * (k + 1), :])
            acc = acc + proj / (1.0 + jnp.exp(-gl))
        y = _dot(acc.astype(BF16), wout_ref[...])
        out_ref[rows, :] = x + mod_ref[:, 2 * d:3 * d] * _rms(y, gpost_ref[...])


def _merge_call(geo, l, n_blocks, x_lat, x_ctx, mods, o_pairs, u, pm, ph, wts):
    r = RB // HALO
    n_halo = geo.n_all // HALO
    up_spec = pl.BlockSpec((HALO, 256), lambda i: (jnp.maximum(i * r - 1, 0), 0))
    un_spec = pl.BlockSpec((HALO, 256), lambda i: (jnp.minimum((i + 1) * r, n_halo - 1), 0))
    o_specs, o_args = [], []
    for o_l, o_c in o_pairs:
        o_specs += [geo.lat_src(o_l), geo.ctx_src(o_c)]
        o_args += [o_l, o_c]
    consts = [wts[k] for k in ("mix_pre_g", "w_gl", "wpool", "d_scale", "w_branch", "w_out", "mix_post_g")]
    return pl.pallas_call(
        functools.partial(_merge_kernel, lat_blocks=geo.lat_blocks, per_b=geo.per_b),
        grid=(n_blocks,),
        in_specs=[geo.lat_src(x_lat), geo.ctx_src(x_ctx), geo.mod(l)] + o_specs
                 + [geo.rows(256), up_spec, un_spec, _resident(pm), _resident(ph)]
                 + [_resident(a, l) for a in consts],
        out_specs=geo.rows(D_MODEL),
        out_shape=jax.ShapeDtypeStruct((n_blocks * RB, D_MODEL), F32),
        compiler_params=_params(),
        name="merge",
    )(x_lat, x_ctx, mods, *o_args, u, u, u, pm, ph, *consts)


def _ffn_kernel(x_ref, mod_ref, gpre_ref, w1_ref, w3_ref, w2_ref, gpost_ref, out_ref):
    d = D_MODEL
    for sub in range(N_SUB):
        rows = slice(TM * sub, TM * (sub + 1))
        x = x_ref[rows, :]
        hb = (_rms(x, gpre_ref[...]) * (1.0 + mod_ref[:, 4 * d:5 * d]) + mod_ref[:, 3 * d:4 * d]).astype(BF16)
        a = _dot(hb, w1_ref[...])
        b = _dot(hb, w3_ref[...])
        act = (a / (1.0 + jnp.exp(-a)) * b).astype(BF16)
        f = _dot(act, w2_ref[...])
        out_ref[rows, :] = x + mod_ref[:, 5 * d:6 * d] * _rms(f, gpost_ref[...])


def _ffn_call(geo, l, x1, mods, wts):
    n, d = x1.shape
    consts = [wts[k] for k in ("ffn_pre_g", "w_ffn1", "w_ffn3", "w_ffn2", "ffn_post_g")]
    return pl.pallas_call(
        _ffn_kernel,
        grid=(n // RB,),
        in_specs=[geo.rows(d), geo.mod(l)] + [_resident(a, l) for a in consts],
        out_specs=geo.rows(d),
        out_shape=jax.ShapeDtypeStruct((n, d), F32),
        compiler_params=_params(),
        name="ffn",
    )(x1, mods, *consts)


def _rope_tables(n_tok):
    rows = n_tok // GRID_W
    row = jnp.repeat(jnp.arange(rows), GRID_W).astype(F32)
    col = jnp.tile(jnp.arange(GRID_W), rows).astype(F32)

    def cos_sin(rot_dim):
        n_freq = rot_dim // 4
        inv = ROPE_THETA ** (-jnp.arange(n_freq, dtype=F32) / n_freq)
        ang = jnp.concatenate([row[:, None] * inv, col[:, None] * inv], axis=-1)
        return jnp.cos(ang), jnp.sin(ang)

    c64, s64 = cos_sin(HEAD_DIM)
    cos_a = jnp.tile(c64, (1, 4))
    sin_a = jnp.concatenate([-s64, -s64, s64, s64], axis=-1)
    c32, s32 = cos_sin(B_ROPE)
    ones = lambda n: jnp.ones((n_tok, n), F32)
    zeros = lambda n: jnp.zeros((n_tok, n), F32)
    na, ra = B_NOPE // 2, B_ROPE // 2
    pad = LANES // 2 - na - ra
    cos_b = jnp.concatenate([ones(na), c32, ones(pad)] * 2, axis=-1)
    sin_b = jnp.concatenate([zeros(na), -s32, zeros(pad), zeros(na), s32, zeros(pad)], axis=-1)
    ident_c = jnp.ones((RB, LANES), F32)
    ident_s = jnp.zeros((RB, LANES), F32)
    return (jnp.concatenate([cos_a, ident_c]), jnp.concatenate([sin_a, ident_s]),
            jnp.concatenate([cos_b, ident_c]), jnp.concatenate([sin_b, ident_s]))


def _pool_matrices():
    i = np.arange(TM)[:, None]
    jm = np.arange(TM)[None, :]
    jh = np.arange(2 * HALO)[None, :]
    halo_pos = np.where(jh < HALO, jh - HALO, TM + jh - HALO)
    pm, ph = [], []
    for win in POOL_WINDOWS:
        lo, hi = win // 2, win - win // 2 - 1
        pm.append(((jm - i >= -lo) & (jm - i <= hi)).astype(np.float32))
        ph.append(((halo_pos - i >= -lo) & (halo_pos - i <= hi)).astype(np.float32))
    return jnp.asarray(np.stack(pm), BF16), jnp.asarray(np.stack(ph), BF16)


def _stacked_weights(mix_pre_g, mix_post_g, ffn_pre_g, ffn_post_g, w_in, a_qn_g, a_kn_g, b_qa_g, b_kva_g,
                     b_w_qb, b_w_kvb, d_w_pool, d_scale, w_branch, w_out, w_ffn1, w_ffn3, w_ffn2):
    depth, d, _ = w_in.shape
    zeros = lambda rows, n: jnp.zeros((depth, rows, n), F32)
    half = HEAD_DIM // 2

    def split_half(w, c):
        return [w[:, :, c:c + half], w[:, :, c + HEAD_DIM:c + HEAD_DIM + half],
                w[:, :, c + half:c + HEAD_DIM], w[:, :, c + HEAD_DIM + half:c + 2 * HEAD_DIM]]

    def nope_rope(nope, rope, rows):
        na, ra = B_NOPE // 2, B_ROPE // 2
        out = []
        for s in range(2):
            out += [zeros(rows, na) if nope is None else nope[:, :, na * s:na * (s + 1)],
                    zeros(rows, ra) if rope is None else rope[:, :, ra * s:ra * (s + 1)],
                    zeros(rows, LANES // 2 - na - ra)]
        return out

    w_small = jnp.concatenate(
        split_half(w_in, 0) + split_half(w_in, 128) + split_half(w_in, 256) + [w_in[:, :, 384:896]]
        + nope_rope(None, w_in[:, :, 896:928], d)
        + split_half(w_in, 928) + split_half(w_in, 1056) + split_half(w_in, 1184) + [w_in[:, :, 1312:1696]], axis=2)
    per_q = B_NOPE + B_ROPE
    per_kv = B_NOPE + B_V
    qcols, kvcols = [], []
    for hd in range(B_HEADS):
        qcols += nope_rope(b_w_qb[:, :, per_q * hd:per_q * hd + B_NOPE],
                           b_w_qb[:, :, per_q * hd + B_NOPE:per_q * (hd + 1)], B_Q_RANK)
        kvcols += nope_rope(b_w_kvb[:, :, per_kv * hd:per_kv * hd + B_NOPE], None, B_KV_RANK)
        kvcols += [b_w_kvb[:, :, per_kv * hd + B_NOPE:per_kv * (hd + 1)], zeros(B_KV_RANK, LANES - B_V)]
    pool_rows = [jnp.concatenate([zeros(POOL_DIM, POOL_DIM * g), d_w_pool[:, g],
                                  zeros(POOL_DIM, POOL_DIM * (POOL_GROUPS - 1 - g))], axis=2)
                 for g in range(POOL_GROUPS)]
    gain = lambda g: g[:, None, :]
    head_gain = lambda g: jnp.concatenate([g[:, :half], g[:, :half], g[:, half:], g[:, half:]], axis=1)
    return {
        "w_small": w_small.astype(BF16),
        "w_gl": w_in[:, :, 1696:].astype(BF16),
        "wqb": jnp.concatenate(qcols, axis=2).astype(BF16),
        "wkvb": jnp.concatenate(kvcols, axis=2).astype(BF16),
        "wpool": jnp.concatenate(pool_rows, axis=1).astype(BF16),
        "w_branch": w_branch.reshape(depth, N_BRANCH * BRANCH_WIDTH, d).astype(BF16),
        "w_out": w_out.astype(BF16),
        "w_ffn1": w_ffn1.astype(BF16), "w_ffn3": w_ffn3.astype(BF16), "w_ffn2": w_ffn2.astype(BF16),
        "mix_pre_g": gain(mix_pre_g), "mix_post_g": gain(mix_post_g),
        "ffn_pre_g": gain(ffn_pre_g), "ffn_post_g": gain(ffn_post_g),
        "a_qn_g": gain(head_gain(a_qn_g)), "a_kn_g": gain(head_gain(a_kn_g)),
        "b_qa_g": gain(b_qa_g), "b_kva_g": gain(b_kva_g), "d_scale": gain(d_scale),
    }


def kernel(x, c, ctx, c_ctx, ada_w, ada_b, mix_pre_g, mix_post_g, ffn_pre_g, ffn_post_g, w_in, a_qn_g, a_kn_g,
           b_qa_g, b_kva_g, b_w_qb, b_w_kvb, c_sink, d_w_pool, d_scale, w_branch, w_out, w_ffn1, w_ffn3, w_ffn2):
    bsz, n_tok, d = x.shape
    depth = ada_w.shape[0]
    assert d == D_MODEL and ctx.shape[1] == CTX_LEN and bsz + 1 <= MOD_ROWS
    geo = _Rows(bsz, n_tok)

    cs = jnp.zeros((MOD_ROWS, d), F32).at[:bsz].set(c).at[bsz].set(c_ctx)
    mods = _mods_call(cs, ada_w, ada_b)[:, :, None, :]
    tabs = _rope_tables(n_tok)
    pm, ph = _pool_matrices()
    wts = _stacked_weights(mix_pre_g, mix_post_g, ffn_pre_g, ffn_post_g, w_in, a_qn_g, a_kn_g, b_qa_g, b_kva_g,
                           b_w_qb, b_w_kvb, d_w_pool, d_scale, w_branch, w_out, w_ffn1, w_ffn3, w_ffn2)

    x_lat, x_ctx = x, ctx
    for l in range(depth):
        with_ctx = l < depth - 1
        n_blocks = geo.all_blocks if with_ctx else geo.lat_blocks
        qa, kva, qb, kb, vb, qc, kvc, u = _pre_call(geo, l, x_lat, x_ctx, mods, wts, tabs)
        oa = _dense_attn_call("attn_a", geo, qa, [kva], _HEADS_A, with_ctx)
        ob = _dense_attn_call("attn_b", geo, qb, [kb, vb], _HEADS_B, with_ctx)
        oc = _window_attn_call(geo, l, c_sink, qc, kvc, with_ctx)
        x1 = _merge_call(geo, l, n_blocks, x_lat, x_ctx, mods, [oa, ob, oc], u, pm, ph, wts)
        x_lat = x_ctx = _ffn_call(geo, l, x1, mods, wts)
    return x_lat.reshape(bsz, n_tok, d)
```

```python
import functools

import jax
import jax.numpy as jnp
from jax import lax
from jax.experimental import pallas as pl
from jax.experimental.pallas import tpu as pltpu

D_MODEL = 1024
CTX_LEN = 256
GRID_W = 64
HEAD_DIM = 64
ROPE_THETA = 10000.0
EPS = 1e-6
N_BRANCH = 4
BRANCH_WIDTH = 256
B_HEADS = 4
B_Q_RANK = 256
B_KV_RANK = 128
B_NOPE = 64
B_ROPE = 32
B_V = 64
WINDOW = 128
POOL_WINDOWS = (2, 4, 8, 16)
POOL_GROUPS = 4
POOL_DIM = 64
LOG2E = 1.4426950408889634

LANES = 128
TM = CTX_LEN
N_SUB = 2
RB = N_SUB * TM
HALO = 8
BAND = 2 * TM
STAGE_GAP = 1
N_SMALL = 1792
MOD_ROWS = 16
VMEM_LIMIT = 56 * 1024 * 1024

F32 = jnp.float32
BF16 = jnp.bfloat16


def _dot(a, b):
    return jnp.dot(a, b, preferred_element_type=F32)


def _dot_t(a, b):
    return lax.dot_general(a, b, (((1,), (1,)), ((), ())), preferred_element_type=F32)


def _rms(x, g):
    return x * lax.rsqrt(jnp.mean(x * x, axis=-1, keepdims=True) + EPS) * g


def _lane_lo(shape, width):
    lane = lax.broadcasted_iota(jnp.int32, shape, 1)
    return (lane % (2 * width)) < width


def _head0(shape):
    return _lane_lo(shape, HEAD_DIM // 2)


def _rope(x, cos, sin):
    return x * cos + pltpu.roll(x, LANES // 2, 1) * sin


def _head_rms(x, g):
    h0 = _head0(x.shape)
    s = x * x
    s0 = jnp.sum(jnp.where(h0, s, 0.0), axis=-1, keepdims=True)
    s1 = jnp.sum(jnp.where(h0, 0.0, s), axis=-1, keepdims=True)
    ms = jnp.where(h0, s0, s1) * (1.0 / HEAD_DIM)
    return x * lax.rsqrt(ms + EPS) * g


def _pair_layouts(k, v):
    h0 = _head0(k.shape)
    lo = _lane_lo(v.shape, HEAD_DIM)
    k_up = pltpu.roll(k, HEAD_DIM // 2, 1)
    k_dn = pltpu.roll(k, LANES - HEAD_DIM // 2, 1)
    vr = pltpu.roll(v, HEAD_DIM, 1)
    return (jnp.where(h0, k, k_up), jnp.where(lo, v, 1.0), jnp.where(h0, k_dn, k), jnp.where(lo, vr, 1.0))


def _resident(arr, layer=None):
    if layer is None:
        nd = arr.ndim
        return pl.BlockSpec(arr.shape, lambda *_: (0,) * nd, pipeline_mode=pl.Buffered(1))
    nd = arr.ndim - 1
    return pl.BlockSpec((None,) + arr.shape[1:], lambda *_: (layer,) + (0,) * nd, pipeline_mode=pl.Buffered(1))


def _params(n_axes=1):
    return pltpu.CompilerParams(dimension_semantics=("arbitrary",) * n_axes, vmem_limit_bytes=VMEM_LIMIT)


class _Rows:
    def __init__(self, bsz, n_tok):
        assert n_tok % RB == 0 and (bsz * CTX_LEN) % RB == 0
        self.bsz, self.n_tok = bsz, n_tok
        self.n_lat = bsz * n_tok
        self.n_all = self.n_lat + bsz * CTX_LEN
        self.per_b = n_tok // RB
        self.lat_blocks = self.n_lat // RB
        self.all_blocks = self.n_all // RB

    def rows(self, width):
        return pl.BlockSpec((RB, width), lambda i: (i, 0))

    def lat_src(self, arr):
        last = self.lat_blocks - 1
        if arr.ndim == 3:
            return pl.BlockSpec((None, RB, arr.shape[2]),
                                lambda i: (jnp.minimum(i, last) // self.per_b, jnp.minimum(i, last) % self.per_b, 0))
        return pl.BlockSpec((RB, arr.shape[1]), lambda i: (jnp.minimum(i, last), 0))

    def ctx_src(self, arr):
        ctx_last = self.all_blocks - self.lat_blocks - 1
        rel = lambda i: jnp.clip(i - self.lat_blocks, 0, ctx_last)
        if arr.ndim == 3:
            return pl.BlockSpec((N_SUB, CTX_LEN, arr.shape[2]), lambda i: (rel(i), 0, 0))
        base = (arr.shape[0] - self.bsz * CTX_LEN) // RB
        return pl.BlockSpec((RB, arr.shape[1]), lambda i: (base + rel(i), 0))

    def mod(self, layer):
        return pl.BlockSpec((None, None, 1, 6 * D_MODEL),
                            lambda i: (layer, jnp.where(i < self.lat_blocks, i // self.per_b, self.bsz), 0, 0))

    def table(self):
        return pl.BlockSpec((RB, LANES), lambda i: (jnp.where(i < self.lat_blocks, i % self.per_b, self.per_b), 0))


def _mods_kernel(c_ref, w_ref, b_ref, o_ref):
    c = c_ref[...]
    s = c / (1.0 + jnp.exp(-c))
    o_ref[...] = _dot(s.astype(BF16), w_ref[...].astype(BF16)) + b_ref[...]


def _mods_call(cs, ada_w, ada_b):
    depth, d, n = ada_w.shape
    nb = 1536
    return pl.pallas_call(
        _mods_kernel,
        grid=(depth, n // nb),
        in_specs=[pl.BlockSpec((MOD_ROWS, d), lambda l, j: (0, 0)),
                  pl.BlockSpec((None, d, nb), lambda l, j: (l, 0, j)),
                  pl.BlockSpec((None, 1, nb), lambda l, j: (l, 0, j))],
        out_specs=pl.BlockSpec((None, MOD_ROWS, nb), lambda l, j: (l, 0, j)),
        out_shape=jax.ShapeDtypeStruct((depth, MOD_ROWS, n), F32),
        compiler_params=_params(2),
        name="adaln_mods",
    )(cs, ada_w, ada_b.reshape(depth, 1, n))


def _sub_rows(ref, sub):
    return ref[sub] if len(ref.shape) == 3 else ref[TM * sub:TM * (sub + 1), :]


def _pre_kernel(xl_ref, xc_ref, mod_ref, g_ref, w_ref, gq_ref, gk_ref, gqa_ref, gkva_ref, wqb_ref, wkvb_ref,
                cosa_ref, sina_ref, cosb_ref, sinb_ref,
                qa_ref, kva_ref, qb_ref, kb_ref, vb_ref, qc_ref, kvc_ref, u_ref, *, lat_blocks):
    d = D_MODEL
    is_ctx = pl.program_id(0) >= lat_blocks
    sc_a = HEAD_DIM ** -0.5 * LOG2E
    sc_b = (B_NOPE + B_ROPE) ** -0.5 * LOG2E
    for sub in range(N_SUB):
        rows = slice(TM * sub, TM * (sub + 1))
        x = jnp.where(is_ctx, _sub_rows(xc_ref, sub), _sub_rows(xl_ref, sub))
        h = _rms(x, g_ref[...]) * (1.0 + mod_ref[:, d:2 * d]) + mod_ref[:, 0:d]
        z = _dot(h.astype(BF16), w_ref[...])
        cosa, sina = cosa_ref[rows, :], sina_ref[rows, :]
        cosb, sinb = cosb_ref[rows, :], sinb_ref[rows, :]

        for j in range(2):
            q = _head_rms(z[:, LANES * j:LANES * (j + 1)], gq_ref[...])
            qa_ref[rows, LANES * j:LANES * (j + 1)] = (_rope(q, cosa, sina) * sc_a).astype(BF16)
        k = _rope(_head_rms(z[:, 256:384], gk_ref[...]), cosa, sina)
        for i, blk in enumerate(_pair_layouts(k, z[:, 384:512])):
            kva_ref[rows, LANES * i:LANES * (i + 1)] = blk.astype(BF16)

        qn = _rms(z[:, 512:768], gqa_ref[...]).astype(BF16)
        qb = _dot(qn, wqb_ref[...]) * sc_b
        kvn = _rms(z[:, 768:896], gkva_ref[...]).astype(BF16)
        kvb = _dot(kvn, wkvb_ref[...])
        kbr = _rope(z[:, 896:1024], cosb, sinb)
        lo = _lane_lo(kbr.shape, HEAD_DIM)
        for hd in range(B_HEADS):
            c0 = LANES * hd
            qb_ref[rows, c0:c0 + LANES] = _rope(qb[:, c0:c0 + LANES], cosb, sinb).astype(BF16)
            kb_ref[rows, c0:c0 + LANES] = (kvb[:, 2 * c0:2 * c0 + LANES] + kbr).astype(BF16)
            vb_ref[rows, c0:c0 + LANES] = jnp.where(lo, kvb[:, 2 * c0 + LANES:2 * c0 + 2 * LANES], 1.0).astype(BF16)

        for j in range(2):
            q = z[:, 1024 + LANES * j:1024 + LANES * (j + 1)]
            qc_ref[rows, LANES * j:LANES * (j + 1)] = (_rope(q, cosa, sina) * sc_a).astype(BF16)
        k = _rope(z[:, 1280:1408], cosa, sina)
        for i, blk in enumerate(_pair_layouts(k, z[:, 1408:1536])):
            kvc_ref[rows, LANES * i:LANES * (i + 1)] = blk.astype(BF16)

        u_ref[rows, :] = z[:, 1536:1792]


def _pre_call(geo, l, x_lat, x_ctx, mods, wts, tabs):
    widths = (256, 512, 512, 512, 512, 256, 512)
    out_shape = [jax.ShapeDtypeStruct((geo.n_all, w), BF16) for w in widths]
    out_shape.append(jax.ShapeDtypeStruct((geo.n_all, 256), F32))
    consts = [wts[k] for k in ("mix_pre_g", "w_small", "a_qn_g", "a_kn_g", "b_qa_g", "b_kva_g", "wqb", "wkvb")]
    return pl.pallas_call(
        functools.partial(_pre_kernel, lat_blocks=geo.lat_blocks),
        grid=(geo.all_blocks,),
        in_specs=[geo.lat_src(x_lat), geo.ctx_src(x_ctx), geo.mod(l)] + [_resident(a, l) for a in consts]
                 + [geo.table()] * 4,
        out_specs=[geo.rows(w) for w in widths] + [geo.rows(256)],
        out_shape=out_shape,
        compiler_params=_params(),
        name="pre_attn",
    )(x_lat, x_ctx, mods, *consts, *tabs)


def _probs(scores, extra=None):
    m = functools.reduce(jnp.maximum, [jnp.max(s, axis=-1, keepdims=True) for s in scores])
    if extra is not None:
        m = jnp.maximum(m, extra)
    ps = [jnp.exp2(s - m).astype(BF16) for s in scores]
    return ps, (None if extra is None else jnp.exp2(extra - m))


def _softmax_rows(s_ref, p_ref, extra=None):
    n = s_ref.shape[1]
    m = s_ref[:, 0:LANES]
    for c in range(1, n // LANES):
        m = jnp.maximum(m, s_ref[:, LANES * c:LANES * (c + 1)])
    m = jnp.max(m, axis=-1, keepdims=True)
    if extra is not None:
        m = jnp.maximum(m, extra)
    mb = jnp.broadcast_to(m, (s_ref.shape[0], LANES))
    for c in range(n // (2 * LANES)):
        cols = slice(2 * LANES * c, 2 * LANES * (c + 1))
        s = s_ref[:, cols]
        p_ref[:, cols] = jnp.concatenate([jnp.exp2(s[:, :LANES] - mb), jnp.exp2(s[:, LANES:] - mb)],
                                         axis=1).astype(BF16)
    return None if extra is None else jnp.exp2(extra - m)


def _normalise_pair(r0, r1, e0=None, e1=None):
    lo = _lane_lo(r0.shape, HEAD_DIM)
    r0r = pltpu.roll(r0, HEAD_DIM, 1)
    r1r = pltpu.roll(r1, HEAD_DIM, 1)
    den0 = r0r if e0 is None else r0r + e0
    den1 = r1 if e1 is None else r1 + e1
    return jnp.where(lo, r0 / den0, r1r / den1)


def _stack_pair(q):
    h0 = _head0(q.shape)
    zero = jnp.zeros_like(q)
    return jnp.concatenate([jnp.where(h0, q, zero), jnp.where(h0, zero, q)], axis=0)


def _dense_attn_kernel(*refs, heads, n_kv, with_ctx):
    q_lat, q_ctx = refs[0:2]
    kv_lat = refs[2:2 + 2 * n_kv:2]
    kv_ctx = refs[3:3 + 2 * n_kv:2]
    o_lat, o_ctx = refs[2 + 2 * n_kv:4 + 2 * n_kv]
    scratch = refs[4 + 2 * n_kv:]
    n_heads = len(heads)
    s_bufs, p_bufs, r_bufs = scratch[0:n_heads], scratch[n_heads:2 * n_heads], scratch[2 * n_heads:]
    n_tok = q_lat.shape[0]
    n_tiles = n_tok // TM

    def q_block(ref, row0, h):
        qcol, mask, *_ = heads[h]
        q = ref[pl.ds(row0, TM), LANES * qcol:LANES * (qcol + 1)]
        if mask is None:
            return q
        h0 = _head0(q.shape)
        return jnp.where(h0 if mask == "head0" else ~h0, q, jnp.zeros_like(q))

    def keys(refs_, h):
        _, _, karr, kcol, _, _ = heads[h]
        return refs_[karr][:, LANES * kcol:LANES * (kcol + 1)]

    def values(refs_, h):
        _, _, _, _, varr, vcol = heads[h]
        return refs_[varr][:, LANES * vcol:LANES * (vcol + 1)]

    def store_pair(ref, row0, j, r0, r1):
        ref[pl.ds(row0, TM), LANES * j:LANES * (j + 1)] = _normalise_pair(r0, r1).astype(BF16)

    if with_ctx:
        rs = []
        for h in range(n_heads):
            (p,), _ = _probs([_dot_t(q_block(q_ctx, 0, h), keys(kv_ctx, h))])
            rs.append(_dot(p, values(kv_ctx, h)))
        for j in range(n_heads // 2):
            store_pair(o_ctx, 0, j, rs[2 * j], rs[2 * j + 1])

    def scores(row0, h, buf):
        q = q_block(q_lat, row0, h)
        s_bufs[buf][:, 0:CTX_LEN] = _dot_t(q, keys(kv_ctx, h))
        s_bufs[buf][:, CTX_LEN:] = _dot_t(q, keys(kv_lat, h))

    def softmax(buf):
        _softmax_rows(s_bufs[buf], p_bufs[buf])

    def weighted(h, buf):
        return (_dot(p_bufs[buf][:, 0:CTX_LEN], values(kv_ctx, h))
                + _dot(p_bufs[buf][:, CTX_LEN:], values(kv_lat, h)))

    for h in range(n_heads - STAGE_GAP, n_heads):
        p_bufs[h][...] = jnp.zeros(p_bufs[h].shape, BF16)
    for rb in r_bufs:
        rb[...] = jnp.zeros(rb.shape, F32)
    for h in range(STAGE_GAP):
        scores(0, h, h)

    def finish(row0, hp):
        r = weighted(hp, hp)
        if hp % 2 == 0:
            r_bufs[hp // 2][...] = r
        else:
            store_pair(o_lat, row0, hp // 2, r_bufs[hp // 2][...], r)

    def tile_body(i, carry):
        row0 = pl.multiple_of(i * TM, TM)
        prev0 = pl.multiple_of(jnp.maximum(i - 1, 0) * TM, TM)
        next0 = pl.multiple_of(jnp.minimum(i + 1, n_tiles - 1) * TM, TM)
        for h in range(n_heads):
            ahead = h + STAGE_GAP
            scores(row0 if ahead < n_heads else next0, ahead % n_heads, ahead % n_heads)
            softmax(h)
            behind = h - STAGE_GAP
            finish(row0 if behind >= 0 else prev0, behind % n_heads)
        return carry

    lax.fori_loop(0, n_tiles, tile_body, 0)
    for h in range(n_heads - STAGE_GAP, n_heads):
        finish(n_tok - TM, h)
    if not with_ctx:
        o_ctx[...] = jnp.zeros(o_ctx.shape, BF16)


_HEADS_A = tuple((h // 2, "head0" if h % 2 == 0 else "head1", 0, 2 * (h // 2), 0, 2 * (h // 2) + 1)
                 for h in range(4))
_HEADS_B = tuple((h, None, 0, h, 1, h) for h in range(4))


def _batch_specs(geo, width):
    ctx0 = geo.n_lat // CTX_LEN
    return [pl.BlockSpec((geo.n_tok, width), lambda b: (b, 0)),
            pl.BlockSpec((CTX_LEN, width), lambda b: (ctx0 + b, 0))]


def _attn_out(geo):
    specs = [pl.BlockSpec((geo.n_tok, BRANCH_WIDTH), lambda b: (b, 0)),
             pl.BlockSpec((CTX_LEN, BRANCH_WIDTH), lambda b: (b, 0))]
    shapes = [jax.ShapeDtypeStruct((geo.n_lat, BRANCH_WIDTH), BF16),
              jax.ShapeDtypeStruct((geo.bsz * CTX_LEN, BRANCH_WIDTH), BF16)]
    return specs, shapes


def _dense_attn_call(name, geo, q, kvs, heads, with_ctx):
    s_len = geo.n_tok + CTX_LEN
    in_specs = _batch_specs(geo, q.shape[1])
    args = [q, q]
    for kv in kvs:
        in_specs += _batch_specs(geo, kv.shape[1])
        args += [kv, kv]
    out_specs, out_shape = _attn_out(geo)
    return pl.pallas_call(
        functools.partial(_dense_attn_kernel, heads=heads, n_kv=len(kvs), with_ctx=with_ctx),
        grid=(geo.bsz,),
        in_specs=in_specs,
        out_specs=out_specs,
        out_shape=out_shape,
        scratch_shapes=[pltpu.VMEM((TM, s_len), F32)] * len(heads) + [pltpu.VMEM((TM, s_len), BF16)] * len(heads)
                       + [pltpu.VMEM((TM, LANES), F32)] * (len(heads) // 2),
        compiler_params=_params(),
        name=name,
    )(*args)


def _window_attn_kernel(sink_ref, q_lat, q_ctx, kv_lat, kv_ctx, o_lat, o_ctx, *scratch, with_ctx, layer):
    s_bufs, p_bufs, e_bufs, bias_ref = scratch[0:4], scratch[4:8], scratch[8:12], scratch[12]
    n_tok = q_lat.shape[0]
    n_tiles = n_tok // TM
    assert n_tiles >= 2

    row = lax.broadcasted_iota(jnp.int32, (2 * TM, BAND), 0)
    col = lax.broadcasted_iota(jnp.int32, (2 * TM, BAND), 1)
    rel0 = col - jnp.where(row < TM, row, row - TM)
    for place, off in enumerate((0, -WINDOW, TM - BAND)):
        bias_ref[place] = jnp.where(jnp.abs(rel0 + off) <= WINDOW, 0.0, -1e30)

    def sink_col(j):
        row = lax.broadcasted_iota(jnp.int32, (2 * TM, 1), 0)
        return jnp.where(row < TM, sink_ref[layer, 2 * j], sink_ref[layer, 2 * j + 1]) * LOG2E

    def cols(j):
        return slice(2 * LANES * j, 2 * LANES * j + LANES), slice(2 * LANES * j + LANES, 2 * LANES * (j + 1))

    def store_pair(ref, row0, j, r, e):
        ref[pl.ds(row0, TM), LANES * j:LANES * (j + 1)] = _normalise_pair(r[:TM], r[TM:], e[:TM], e[TM:]).astype(BF16)

    if with_ctx:
        for j in range(2):
            c_kk, c_v1 = cols(j)
            (p,), e = _probs([_dot_t(_stack_pair(q_ctx[:, LANES * j:LANES * (j + 1)]), kv_ctx[:, c_kk])], sink_col(j))
            store_pair(o_ctx, 0, j, _dot(p, kv_ctx[:, c_v1]), e)

    def band_start(i):
        if isinstance(i, int):
            return min(max(i * TM - WINDOW, 0), n_tok - BAND)
        return pl.multiple_of(jnp.clip(i * TM - WINDOW, 0, n_tok - BAND), WINDOW)

    def scores(i, j, buf):
        c_kk, _ = cols(j)
        row0 = i * TM if isinstance(i, int) else pl.multiple_of(i * TM, TM)
        qs = _stack_pair(q_lat[pl.ds(row0, TM), LANES * j:LANES * (j + 1)])
        s_bufs[buf][:, 0:CTX_LEN] = _dot_t(qs, kv_ctx[:, c_kk])
        if isinstance(i, int):
            place = 0 if i == 0 else 2 if i == n_tiles - 1 else 1
        else:
            place = jnp.where(i == 0, 0, jnp.where(i == n_tiles - 1, 2, 1))
        s_bufs[buf][:, CTX_LEN:] = _dot_t(qs, kv_lat[pl.ds(band_start(i), BAND), c_kk]) + bias_ref[place]

    def softmax(i, j, buf):
        e = _softmax_rows(s_bufs[buf], p_bufs[buf], sink_col(j))
        e_bufs[buf][...] = jnp.broadcast_to(e, e_bufs[buf].shape)

    def weighted(i, j, buf):
        _, c_v1 = cols(j)
        return (_dot(p_bufs[buf][:, 0:CTX_LEN], kv_ctx[:, c_v1])
                + _dot(p_bufs[buf][:, CTX_LEN:], kv_lat[pl.ds(band_start(i), BAND), c_v1]))

    def finish(i, j, buf):
        row0 = i * TM if isinstance(i, int) else pl.multiple_of(i * TM, TM)
        store_pair(o_lat, row0, j, weighted(i, j, buf), e_bufs[buf][...])

    def pair_of_tiles(i0, first, last):
        i1 = i0 + 1
        for c, (i, j) in enumerate(((i0, 0), (i0, 1), (i1, 0), (i1, 1))):
            if c < 2:
                scores(i1, j, c + 2)
            elif not last:
                scores(i0 + 2, j, c - 2)
            softmax(i, j, c)
            if c >= 2:
                finish(i0, j, c - 2)
            elif not first:
                finish(i0 - 1, j, c + 2)

    assert n_tiles % 2 == 0 and n_tiles >= 4
    scores(0, 0, 0)
    scores(0, 1, 1)
    pair_of_tiles(0, True, False)

    def body(k, carry):
        pair_of_tiles(2 * k, False, False)
        return carry

    lax.fori_loop(1, n_tiles // 2 - 1, body, 0)
    pair_of_tiles(n_tiles - 2, False, True)
    finish(n_tiles - 1, 0, 2)
    finish(n_tiles - 1, 1, 3)
    if not with_ctx:
        o_ctx[...] = jnp.zeros(o_ctx.shape, BF16)


def _window_attn_call(geo, l, sink, q, kv, with_ctx):
    n_keys = CTX_LEN + BAND
    out_specs, out_shape = _attn_out(geo)
    return pl.pallas_call(
        functools.partial(_window_attn_kernel, with_ctx=with_ctx, layer=l),
        grid=(geo.bsz,),
        in_specs=[pl.BlockSpec(memory_space=pltpu.SMEM)] + _batch_specs(geo, q.shape[1]) + _batch_specs(geo, kv.shape[1]),
        out_specs=out_specs,
        out_shape=out_shape,
        scratch_shapes=[pltpu.VMEM((2 * TM, n_keys), F32)] * 4 + [pltpu.VMEM((2 * TM, n_keys), BF16)] * 4
                       + [pltpu.VMEM((2 * TM, LANES), F32)] * 4 + [pltpu.VMEM((3, 2 * TM, BAND), F32)],
        compiler_params=_params(),
        name="attn_c",
    )(sink, q, q, kv, kv)


def _window_sums(ext):
    assert POOL_WINDOWS == (2, 4, 8, 16) and HALO >= 8
    n = ext.shape[0]
    back = lambda x, k: pltpu.roll(x, k, 0)
    ahead = lambda x, k: pltpu.roll(x, n - k, 0)
    e0, e1 = ext[:, 0:LANES], ext[:, LANES:2 * LANES]
    lo = _lane_lo(e0.shape, POOL_DIM)
    a2 = e0 + back(e0, 1)
    a4 = a2 + back(a2, 2)
    col0 = jnp.where(lo, a2, ahead(a4, 1))
    b2 = e1 + back(e1, 1)
    b4 = b2 + back(b2, 2)
    b8 = b4 + back(b4, 4)
    b16 = b8 + back(b8, 8)
    col1 = jnp.where(lo, ahead(b8, 3), ahead(b16, 7))
    return jnp.concatenate([col0[HALO:HALO + TM], col1[HALO:HALO + TM]], axis=1)


def _merge_kernel(xl_ref, xc_ref, mod_ref, oal_ref, oac_ref, obl_ref, obc_ref, ocl_ref, occ_ref,
                  u_ref, up_ref, un_ref, gpre_ref, wgl_ref, wpool_ref, dscale_ref,
                  wbr_ref, wout_ref, gpost_ref, out_ref, *, lat_blocks, per_b):
    d = D_MODEL
    i = pl.program_id(0)
    is_ctx = i >= lat_blocks
    blk = i % per_b
    lane = lax.broadcasted_iota(jnp.int32, (TM, 2 * LANES), 1)
    grp = lane // POOL_DIM
    row = lax.broadcasted_iota(jnp.int32, (TM, 2 * LANES), 0)
    p_lo = jnp.left_shift(1, grp)
    n_tok = jnp.where(is_ctx, CTX_LEN, per_b * RB)
    zero_halo = jnp.zeros((HALO, 2 * LANES), F32)

    for sub in range(N_SUB):
        rows = slice(TM * sub, TM * (sub + 1))
        x = jnp.where(is_ctx, _sub_rows(xc_ref, sub), _sub_rows(xl_ref, sub))
        hb = (_rms(x, gpre_ref[...]) * (1.0 + mod_ref[:, d:2 * d]) + mod_ref[:, 0:d]).astype(BF16)

        u = u_ref[rows, :]
        if sub == 0:
            prev = jnp.where(jnp.logical_or(is_ctx, blk == 0), zero_halo, up_ref[...])
        else:
            prev = jnp.where(is_ctx, zero_halo, u_ref[TM * sub - HALO:TM * sub, :])
        if sub == N_SUB - 1:
            nxt = jnp.where(jnp.logical_or(is_ctx, blk == per_b - 1), zero_halo, un_ref[...])
        else:
            nxt = jnp.where(is_ctx, zero_halo, u_ref[TM * (sub + 1):TM * (sub + 1) + HALO, :])
        sums = _window_sums(jnp.concatenate([prev, u, nxt], axis=0))
        pos = jnp.where(is_ctx, row, blk * RB + TM * sub + row)
        cnt = jnp.minimum(pos + p_lo, n_tok) - jnp.maximum(pos - p_lo, 0)
        diff = sums / cnt.astype(F32) - u
        o_d = _dot(diff.astype(BF16), wpool_ref[...]) * dscale_ref[...]

        branches = (jnp.where(is_ctx, oac_ref[rows, :], oal_ref[rows, :]),
                    jnp.where(is_ctx, obc_ref[rows, :], obl_ref[rows, :]),
                    jnp.where(is_ctx, occ_ref[rows, :], ocl_ref[rows, :]),
                    o_d.astype(BF16))
        acc = jnp.zeros((TM, d), F32)
        for k in range(N_BRANCH):
            gl = _dot(hb, wgl_ref[:, d * k:d * (k + 1)])
            proj = _dot(branches[k], wbr_ref[BRANCH_WIDTH * k:BRANCH_WIDTH * (k + 1), :])
            acc = acc + proj / (1.0 + jnp.exp(-gl))
        y = _dot(acc.astype(BF16), wout_ref[...])
        out_ref[rows, :] = x + mod_ref[:, 2 * d:3 * d] * _rms(y, gpost_ref[...])


def _merge_call(geo, l, n_blocks, x_lat, x_ctx, mods, o_pairs, u, wts):
    r = RB // HALO
    n_halo = geo.n_all // HALO
    up_spec = pl.BlockSpec((HALO, 256), lambda i: (jnp.maximum(i * r - 1, 0), 0))
    un_spec = pl.BlockSpec((HALO, 256), lambda i: (jnp.minimum((i + 1) * r, n_halo - 1), 0))
    o_specs, o_args = [], []
    for o_l, o_c in o_pairs:
        o_specs += [geo.lat_src(o_l), geo.ctx_src(o_c)]
        o_args += [o_l, o_c]
    consts = [wts[k] for k in ("mix_pre_g", "w_gl", "wpool", "d_scale", "w_branch", "w_out", "mix_post_g")]
    return pl.pallas_call(
        functools.partial(_merge_kernel, lat_blocks=geo.lat_blocks, per_b=geo.per_b),
        grid=(n_blocks,),
        in_specs=[geo.lat_src(x_lat), geo.ctx_src(x_ctx), geo.mod(l)] + o_specs
                 + [geo.rows(256), up_spec, un_spec] + [_resident(a, l) for a in consts],
        out_specs=geo.rows(D_MODEL),
        out_shape=jax.ShapeDtypeStruct((n_blocks * RB, D_MODEL), F32),
        compiler_params=_params(),
        name="merge",
    )(x_lat, x_ctx, mods, *o_args, u, u, u, *consts)


def _ffn_kernel(x_ref, mod_ref, gpre_ref, w1_ref, w3_ref, w2_ref, gpost_ref, out_ref):
    d = D_MODEL
    for sub in range(N_SUB):
        rows = slice(TM * sub, TM * (sub + 1))
        x = x_ref[rows, :]
        hb = (_rms(x, gpre_ref[...]) * (1.0 + mod_ref[:, 4 * d:5 * d]) + mod_ref[:, 3 * d:4 * d]).astype(BF16)
        a = _dot(hb, w1_ref[...])
        b = _dot(hb, w3_ref[...])
        act = (a / (1.0 + jnp.exp(-a)) * b).astype(BF16)
        f = _dot(act, w2_ref[...])
        out_ref[rows, :] = x + mod_ref[:, 5 * d:6 * d] * _rms(f, gpost_ref[...])


def _ffn_call(geo, l, x1, mods, wts):
    n, d = x1.shape
    consts = [wts[k] for k in ("ffn_pre_g", "w_ffn1", "w_ffn3", "w_ffn2", "ffn_post_g")]
    return pl.pallas_call(
        _ffn_kernel,
        grid=(n // RB,),
        in_specs=[geo.rows(d), geo.mod(l)] + [_resident(a, l) for a in consts],
        out_specs=geo.rows(d),
        out_shape=jax.ShapeDtypeStruct((n, d), F32),
        compiler_params=_params(),
        name="ffn",
    )(x1, mods, *consts)


def _rope_tables(n_tok):
    rows = n_tok // GRID_W
    row = jnp.repeat(jnp.arange(rows), GRID_W).astype(F32)
    col = jnp.tile(jnp.arange(GRID_W), rows).astype(F32)

    def cos_sin(rot_dim):
        n_freq = rot_dim // 4
        inv = ROPE_THETA ** (-jnp.arange(n_freq, dtype=F32) / n_freq)
        ang = jnp.concatenate([row[:, None] * inv, col[:, None] * inv], axis=-1)
        return jnp.cos(ang), jnp.sin(ang)

    c64, s64 = cos_sin(HEAD_DIM)
    cos_a = jnp.tile(c64, (1, 4))
    sin_a = jnp.concatenate([-s64, -s64, s64, s64], axis=-1)
    c32, s32 = cos_sin(B_ROPE)
    ones = lambda n: jnp.ones((n_tok, n), F32)
    zeros = lambda n: jnp.zeros((n_tok, n), F32)
    na, ra = B_NOPE // 2, B_ROPE // 2
    pad = LANES // 2 - na - ra
    cos_b = jnp.concatenate([ones(na), c32, ones(pad)] * 2, axis=-1)
    sin_b = jnp.concatenate([zeros(na), -s32, zeros(pad), zeros(na), s32, zeros(pad)], axis=-1)
    ident_c = jnp.ones((RB, LANES), F32)
    ident_s = jnp.zeros((RB, LANES), F32)
    return (jnp.concatenate([cos_a, ident_c]), jnp.concatenate([sin_a, ident_s]),
            jnp.concatenate([cos_b, ident_c]), jnp.concatenate([sin_b, ident_s]))


def _stacked_weights(mix_pre_g, mix_post_g, ffn_pre_g, ffn_post_g, w_in, a_qn_g, a_kn_g, b_qa_g, b_kva_g,
                     b_w_qb, b_w_kvb, d_w_pool, d_scale, w_branch, w_out, w_ffn1, w_ffn3, w_ffn2):
    depth, d, _ = w_in.shape
    zeros = lambda rows, n: jnp.zeros((depth, rows, n), F32)
    half = HEAD_DIM // 2

    def split_half(w, c):
        return [w[:, :, c:c + half], w[:, :, c + HEAD_DIM:c + HEAD_DIM + half],
                w[:, :, c + half:c + HEAD_DIM], w[:, :, c + HEAD_DIM + half:c + 2 * HEAD_DIM]]

    def nope_rope(nope, rope, rows):
        na, ra = B_NOPE // 2, B_ROPE // 2
        out = []
        for s in range(2):
            out += [zeros(rows, na) if nope is None else nope[:, :, na * s:na * (s + 1)],
                    zeros(rows, ra) if rope is None else rope[:, :, ra * s:ra * (s + 1)],
                    zeros(rows, LANES // 2 - na - ra)]
        return out

    w_small = jnp.concatenate(
        split_half(w_in, 0) + split_half(w_in, 128) + split_half(w_in, 256) + [w_in[:, :, 384:896]]
        + nope_rope(None, w_in[:, :, 896:928], d)
        + split_half(w_in, 928) + split_half(w_in, 1056) + split_half(w_in, 1184) + [w_in[:, :, 1312:1696]], axis=2)
    per_q = B_NOPE + B_ROPE
    per_kv = B_NOPE + B_V
    qcols, kvcols = [], []
    for hd in range(B_HEADS):
        qcols += nope_rope(b_w_qb[:, :, per_q * hd:per_q * hd + B_NOPE],
                           b_w_qb[:, :, per_q * hd + B_NOPE:per_q * (hd + 1)], B_Q_RANK)
        kvcols += nope_rope(b_w_kvb[:, :, per_kv * hd:per_kv * hd + B_NOPE], None, B_KV_RANK)
        kvcols += [b_w_kvb[:, :, per_kv * hd + B_NOPE:per_kv * (hd + 1)], zeros(B_KV_RANK, LANES - B_V)]
    pool_rows = [jnp.concatenate([zeros(POOL_DIM, POOL_DIM * g), d_w_pool[:, g],
                                  zeros(POOL_DIM, POOL_DIM * (POOL_GROUPS - 1 - g))], axis=2)
                 for g in range(POOL_GROUPS)]
    gain = lambda g: g[:, None, :]
    head_gain = lambda g: jnp.concatenate([g[:, :half], g[:, :half], g[:, half:], g[:, half:]], axis=1)
    return {
        "w_small": w_small.astype(BF16),
        "w_gl": w_in[:, :, 1696:].astype(BF16),
        "wqb": jnp.concatenate(qcols, axis=2).astype(BF16),
        "wkvb": jnp.concatenate(kvcols, axis=2).astype(BF16),
        "wpool": jnp.concatenate(pool_rows, axis=1).astype(BF16),
        "w_branch": w_branch.reshape(depth, N_BRANCH * BRANCH_WIDTH, d).astype(BF16),
        "w_out": w_out.astype(BF16),
        "w_ffn1": w_ffn1.astype(BF16), "w_ffn3": w_ffn3.astype(BF16), "w_ffn2": w_ffn2.astype(BF16),
        "mix_pre_g": gain(mix_pre_g), "mix_post_g": gain(mix_post_g),
        "ffn_pre_g": gain(ffn_pre_g), "ffn_post_g": gain(ffn_post_g),
        "a_qn_g": gain(head_gain(a_qn_g)), "a_kn_g": gain(head_gain(a_kn_g)),
        "b_qa_g": gain(b_qa_g), "b_kva_g": gain(b_kva_g), "d_scale": gain(d_scale),
    }


def kernel(x, c, ctx, c_ctx, ada_w, ada_b, mix_pre_g, mix_post_g, ffn_pre_g, ffn_post_g, w_in, a_qn_g, a_kn_g,
           b_qa_g, b_kva_g, b_w_qb, b_w_kvb, c_sink, d_w_pool, d_scale, w_branch, w_out, w_ffn1, w_ffn3, w_ffn2):
    bsz, n_tok, d = x.shape
    depth = ada_w.shape[0]
    assert d == D_MODEL and ctx.shape[1] == CTX_LEN and bsz + 1 <= MOD_ROWS
    geo = _Rows(bsz, n_tok)

    cs = jnp.zeros((MOD_ROWS, d), F32).at[:bsz].set(c).at[bsz].set(c_ctx)
    mods = _mods_call(cs, ada_w, ada_b)[:, :, None, :]
    tabs = _rope_tables(n_tok)
    wts = _stacked_weights(mix_pre_g, mix_post_g, ffn_pre_g, ffn_post_g, w_in, a_qn_g, a_kn_g, b_qa_g, b_kva_g,
                           b_w_qb, b_w_kvb, d_w_pool, d_scale, w_branch, w_out, w_ffn1, w_ffn3, w_ffn2)

    x_lat, x_ctx = x, ctx
    for l in range(depth):
        with_ctx = l < depth - 1
        n_blocks = geo.all_blocks if with_ctx else geo.lat_blocks
        qa, kva, qb, kb, vb, qc, kvc, u = _pre_call(geo, l, x_lat, x_ctx, mods, wts, tabs)
        oa = _dense_attn_call("attn_a", geo, qa, [kva], _HEADS_A, with_ctx)
        ob = _dense_attn_call("attn_b", geo, qb, [kb, vb], _HEADS_B, with_ctx)
        oc = _window_attn_call(geo, l, c_sink, qc, kvc, with_ctx)
        x1 = _merge_call(geo, l, n_blocks, x_lat, x_ctx, mods, [oa, ob, oc], u, wts)
        x_lat = x_ctx = _ffn_call(geo, l, x1, mods, wts)
    return x_lat.reshape(bsz, n_tok, d)
```

```python
import functools

import jax
import jax.numpy as jnp
from jax import lax
from jax.experimental import pallas as pl
from jax.experimental.pallas import tpu as pltpu

D_MODEL = 1024
CTX_LEN = 256
GRID_W = 64
HEAD_DIM = 64
ROPE_THETA = 10000.0
EPS = 1e-6
N_BRANCH = 4
BRANCH_WIDTH = 256
B_HEADS = 4
B_Q_RANK = 256
B_KV_RANK = 128
B_NOPE = 64
B_ROPE = 32
B_V = 64
WINDOW = 128
POOL_WINDOWS = (2, 4, 8, 16)
POOL_GROUPS = 4
POOL_DIM = 64
LOG2E = 1.4426950408889634

LANES = 128
TM = CTX_LEN
N_SUB = 4
RB = N_SUB * TM
HALO = 8
BAND = 2 * TM
STAGE_GAP = 1
N_SMALL = 1792
MOD_ROWS = 16
VMEM_LIMIT = 56 * 1024 * 1024

F32 = jnp.float32
BF16 = jnp.bfloat16


def _dot(a, b):
    return jnp.dot(a, b, preferred_element_type=F32)


def _dot_t(a, b):
    return lax.dot_general(a, b, (((1,), (1,)), ((), ())), preferred_element_type=F32)


def _rms(x, g):
    return x * lax.rsqrt(jnp.mean(x * x, axis=-1, keepdims=True) + EPS) * g


def _lane_lo(shape, width):
    lane = lax.broadcasted_iota(jnp.int32, shape, 1)
    return (lane % (2 * width)) < width


def _head0(shape):
    return _lane_lo(shape, HEAD_DIM // 2)


def _rope(x, cos, sin):
    return x * cos + pltpu.roll(x, LANES // 2, 1) * sin


def _head_rms(x, g):
    h0 = _head0(x.shape)
    s = x * x
    s0 = jnp.sum(jnp.where(h0, s, 0.0), axis=-1, keepdims=True)
    s1 = jnp.sum(jnp.where(h0, 0.0, s), axis=-1, keepdims=True)
    ms = jnp.where(h0, s0, s1) * (1.0 / HEAD_DIM)
    return x * lax.rsqrt(ms + EPS) * g


def _pair_layouts(k, v):
    h0 = _head0(k.shape)
    lo = _lane_lo(v.shape, HEAD_DIM)
    k_up = pltpu.roll(k, HEAD_DIM // 2, 1)
    k_dn = pltpu.roll(k, LANES - HEAD_DIM // 2, 1)
    vr = pltpu.roll(v, HEAD_DIM, 1)
    return (jnp.where(h0, k, k_up), jnp.where(lo, v, 1.0), jnp.where(h0, k_dn, k), jnp.where(lo, vr, 1.0))


def _resident(arr, layer=None):
    if layer is None:
        nd = arr.ndim
        return pl.BlockSpec(arr.shape, lambda *_: (0,) * nd, pipeline_mode=pl.Buffered(1))
    nd = arr.ndim - 1
    return pl.BlockSpec((None,) + arr.shape[1:], lambda *_: (layer,) + (0,) * nd, pipeline_mode=pl.Buffered(1))


def _params(n_axes=1):
    return pltpu.CompilerParams(dimension_semantics=("arbitrary",) * n_axes, vmem_limit_bytes=VMEM_LIMIT)


class _Rows:
    def __init__(self, bsz, n_tok):
        assert n_tok % RB == 0 and (bsz * CTX_LEN) % RB == 0
        self.bsz, self.n_tok = bsz, n_tok
        self.n_lat = bsz * n_tok
        self.n_all = self.n_lat + bsz * CTX_LEN
        self.per_b = n_tok // RB
        self.lat_blocks = self.n_lat // RB
        self.all_blocks = self.n_all // RB

    def rows(self, width):
        return pl.BlockSpec((RB, width), lambda i: (i, 0))

    def lat_src(self, arr):
        last = self.lat_blocks - 1
        if arr.ndim == 3:
            return pl.BlockSpec((None, RB, arr.shape[2]),
                                lambda i: (jnp.minimum(i, last) // self.per_b, jnp.minimum(i, last) % self.per_b, 0))
        return pl.BlockSpec((RB, arr.shape[1]), lambda i: (jnp.minimum(i, last), 0))

    def ctx_src(self, arr):
        ctx_last = self.all_blocks - self.lat_blocks - 1
        rel = lambda i: jnp.clip(i - self.lat_blocks, 0, ctx_last)
        if arr.ndim == 3:
            return pl.BlockSpec((N_SUB, CTX_LEN, arr.shape[2]), lambda i: (rel(i), 0, 0))
        base = (arr.shape[0] - self.bsz * CTX_LEN) // RB
        return pl.BlockSpec((RB, arr.shape[1]), lambda i: (base + rel(i), 0))

    def mod(self, layer):
        return pl.BlockSpec((None, None, 1, 6 * D_MODEL),
                            lambda i: (layer, jnp.where(i < self.lat_blocks, i // self.per_b, self.bsz), 0, 0))

    def table(self):
        return pl.BlockSpec((RB, LANES), lambda i: (jnp.where(i < self.lat_blocks, i % self.per_b, self.per_b), 0))


def _mods_kernel(c_ref, w_ref, b_ref, o_ref):
    c = c_ref[...]
    s = c / (1.0 + jnp.exp(-c))
    o_ref[...] = _dot(s.astype(BF16), w_ref[...].astype(BF16)) + b_ref[...]


def _mods_call(cs, ada_w, ada_b):
    depth, d, n = ada_w.shape
    nb = 1536
    return pl.pallas_call(
        _mods_kernel,
        grid=(depth, n // nb),
        in_specs=[pl.BlockSpec((MOD_ROWS, d), lambda l, j: (0, 0)),
                  pl.BlockSpec((None, d, nb), lambda l, j: (l, 0, j)),
                  pl.BlockSpec((None, 1, nb), lambda l, j: (l, 0, j))],
        out_specs=pl.BlockSpec((None, MOD_ROWS, nb), lambda l, j: (l, 0, j)),
        out_shape=jax.ShapeDtypeStruct((depth, MOD_ROWS, n), F32),
        compiler_params=_params(2),
        name="adaln_mods",
    )(cs, ada_w, ada_b.reshape(depth, 1, n))


def _sub_rows(ref, sub):
    return ref[sub] if len(ref.shape) == 3 else ref[TM * sub:TM * (sub + 1), :]


def _pre_kernel(xl_ref, xc_ref, mod_ref, g_ref, w_ref, gq_ref, gk_ref, gqa_ref, gkva_ref, wqb_ref, wkvb_ref,
                cosa_ref, sina_ref, cosb_ref, sinb_ref,
                qa_ref, kva_ref, qb_ref, kb_ref, vb_ref, qc_ref, kvc_ref, u_ref, *, lat_blocks):
    d = D_MODEL
    is_ctx = pl.program_id(0) >= lat_blocks
    sc_a = HEAD_DIM ** -0.5 * LOG2E
    sc_b = (B_NOPE + B_ROPE) ** -0.5 * LOG2E
    for sub in range(N_SUB):
        rows = slice(TM * sub, TM * (sub + 1))
        x = jnp.where(is_ctx, _sub_rows(xc_ref, sub), _sub_rows(xl_ref, sub))
        h = _rms(x, g_ref[...]) * (1.0 + mod_ref[:, d:2 * d]) + mod_ref[:, 0:d]
        z = _dot(h.astype(BF16), w_ref[...])
        cosa, sina = cosa_ref[rows, :], sina_ref[rows, :]
        cosb, sinb = cosb_ref[rows, :], sinb_ref[rows, :]

        for j in range(2):
            q = _head_rms(z[:, LANES * j:LANES * (j + 1)], gq_ref[...])
            qa_ref[rows, LANES * j:LANES * (j + 1)] = (_rope(q, cosa, sina) * sc_a).astype(BF16)
        k = _rope(_head_rms(z[:, 256:384], gk_ref[...]), cosa, sina)
        for i, blk in enumerate(_pair_layouts(k, z[:, 384:512])):
            kva_ref[rows, LANES * i:LANES * (i + 1)] = blk.astype(BF16)

        qn = _rms(z[:, 512:768], gqa_ref[...]).astype(BF16)
        qb = _dot(qn, wqb_ref[...]) * sc_b
        kvn = _rms(z[:, 768:896], gkva_ref[...]).astype(BF16)
        kvb = _dot(kvn, wkvb_ref[...])
        kbr = _rope(z[:, 896:1024], cosb, sinb)
        lo = _lane_lo(kbr.shape, HEAD_DIM)
        for hd in range(B_HEADS):
            c0 = LANES * hd
            qb_ref[rows, c0:c0 + LANES] = _rope(qb[:, c0:c0 + LANES], cosb, sinb).astype(BF16)
            kb_ref[rows, c0:c0 + LANES] = (kvb[:, 2 * c0:2 * c0 + LANES] + kbr).astype(BF16)
            vb_ref[rows, c0:c0 + LANES] = jnp.where(lo, kvb[:, 2 * c0 + LANES:2 * c0 + 2 * LANES], 1.0).astype(BF16)

        for j in range(2):
            q = z[:, 1024 + LANES * j:1024 + LANES * (j + 1)]
            qc_ref[rows, LANES * j:LANES * (j + 1)] = (_rope(q, cosa, sina) * sc_a).astype(BF16)
        k = _rope(z[:, 1280:1408], cosa, sina)
        for i, blk in enumerate(_pair_layouts(k, z[:, 1408:1536])):
            kvc_ref[rows, LANES * i:LANES * (i + 1)] = blk.astype(BF16)

        u_ref[rows, :] = z[:, 1536:1792]


def _pre_call(geo, l, x_lat, x_ctx, mods, wts, tabs):
    widths = (256, 512, 512, 512, 512, 256, 512)
    out_shape = [jax.ShapeDtypeStruct((geo.n_all, w), BF16) for w in widths]
    out_shape.append(jax.ShapeDtypeStruct((geo.n_all, 256), F32))
    consts = [wts[k] for k in ("mix_pre_g", "w_small", "a_qn_g", "a_kn_g", "b_qa_g", "b_kva_g", "wqb", "wkvb")]
    return pl.pallas_call(
        functools.partial(_pre_kernel, lat_blocks=geo.lat_blocks),
        grid=(geo.all_blocks,),
        in_specs=[geo.lat_src(x_lat), geo.ctx_src(x_ctx), geo.mod(l)] + [_resident(a, l) for a in consts]
                 + [geo.table()] * 4,
        out_specs=[geo.rows(w) for w in widths] + [geo.rows(256)],
        out_shape=out_shape,
        compiler_params=_params(),
        name="pre_attn",
    )(x_lat, x_ctx, mods, *consts, *tabs)


def _probs(scores, extra=None):
    m = functools.reduce(jnp.maximum, [jnp.max(s, axis=-1, keepdims=True) for s in scores])
    if extra is not None:
        m = jnp.maximum(m, extra)
    ps = [jnp.exp2(s - m).astype(BF16) for s in scores]
    return ps, (None if extra is None else jnp.exp2(extra - m))


def _softmax_rows(s_ref, p_ref, extra=None):
    n = s_ref.shape[1]
    m = s_ref[:, 0:LANES]
    for c in range(1, n // LANES):
        m = jnp.maximum(m, s_ref[:, LANES * c:LANES * (c + 1)])
    m = jnp.max(m, axis=-1, keepdims=True)
    if extra is not None:
        m = jnp.maximum(m, extra)
    mb = jnp.broadcast_to(m, (s_ref.shape[0], LANES))
    for c in range(n // (2 * LANES)):
        cols = slice(2 * LANES * c, 2 * LANES * (c + 1))
        s = s_ref[:, cols]
        p_ref[:, cols] = jnp.concatenate([jnp.exp2(s[:, :LANES] - mb), jnp.exp2(s[:, LANES:] - mb)],
                                         axis=1).astype(BF16)
    return None if extra is None else jnp.exp2(extra - m)


def _normalise_pair(r0, r1, e0=None, e1=None):
    lo = _lane_lo(r0.shape, HEAD_DIM)
    r0r = pltpu.roll(r0, HEAD_DIM, 1)
    r1r = pltpu.roll(r1, HEAD_DIM, 1)
    den0 = r0r if e0 is None else r0r + e0
    den1 = r1 if e1 is None else r1 + e1
    return jnp.where(lo, r0 / den0, r1r / den1)


def _stack_pair(q):
    h0 = _head0(q.shape)
    zero = jnp.zeros_like(q)
    return jnp.concatenate([jnp.where(h0, q, zero), jnp.where(h0, zero, q)], axis=0)


def _dense_attn_kernel(*refs, heads, n_kv, with_ctx):
    q_lat, q_ctx = refs[0:2]
    kv_lat = refs[2:2 + 2 * n_kv:2]
    kv_ctx = refs[3:3 + 2 * n_kv:2]
    o_lat, o_ctx = refs[2 + 2 * n_kv:4 + 2 * n_kv]
    scratch = refs[4 + 2 * n_kv:]
    n_heads = len(heads)
    s_bufs, p_bufs, r_bufs = scratch[0:n_heads], scratch[n_heads:2 * n_heads], scratch[2 * n_heads:]
    n_tok = q_lat.shape[0]
    n_tiles = n_tok // TM

    def q_block(ref, row0, h):
        qcol, mask, *_ = heads[h]
        q = ref[pl.ds(row0, TM), LANES * qcol:LANES * (qcol + 1)]
        if mask is None:
            return q
        h0 = _head0(q.shape)
        return jnp.where(h0 if mask == "head0" else ~h0, q, jnp.zeros_like(q))

    def keys(refs_, h):
        _, _, karr, kcol, _, _ = heads[h]
        return refs_[karr][:, LANES * kcol:LANES * (kcol + 1)]

    def values(refs_, h):
        _, _, _, _, varr, vcol = heads[h]
        return refs_[varr][:, LANES * vcol:LANES * (vcol + 1)]

    def store_pair(ref, row0, j, r0, r1):
        ref[pl.ds(row0, TM), LANES * j:LANES * (j + 1)] = _normalise_pair(r0, r1).astype(BF16)

    if with_ctx:
        rs = []
        for h in range(n_heads):
            (p,), _ = _probs([_dot_t(q_block(q_ctx, 0, h), keys(kv_ctx, h))])
            rs.append(_dot(p, values(kv_ctx, h)))
        for j in range(n_heads // 2):
            store_pair(o_ctx, 0, j, rs[2 * j], rs[2 * j + 1])

    def scores(row0, h, buf):
        q = q_block(q_lat, row0, h)
        s_bufs[buf][:, 0:CTX_LEN] = _dot_t(q, keys(kv_ctx, h))
        s_bufs[buf][:, CTX_LEN:] = _dot_t(q, keys(kv_lat, h))

    def softmax(buf):
        _softmax_rows(s_bufs[buf], p_bufs[buf])

    def weighted(h, buf):
        return (_dot(p_bufs[buf][:, 0:CTX_LEN], values(kv_ctx, h))
                + _dot(p_bufs[buf][:, CTX_LEN:], values(kv_lat, h)))

    for h in range(n_heads - STAGE_GAP, n_heads):
        p_bufs[h][...] = jnp.zeros(p_bufs[h].shape, BF16)
    for rb in r_bufs:
        rb[...] = jnp.zeros(rb.shape, F32)
    for h in range(STAGE_GAP):
        scores(0, h, h)

    def finish(row0, hp):
        r = weighted(hp, hp)
        if hp % 2 == 0:
            r_bufs[hp // 2][...] = r
        else:
            store_pair(o_lat, row0, hp // 2, r_bufs[hp // 2][...], r)

    def tile_body(i, carry):
        row0 = pl.multiple_of(i * TM, TM)
        prev0 = pl.multiple_of(jnp.maximum(i - 1, 0) * TM, TM)
        next0 = pl.multiple_of(jnp.minimum(i + 1, n_tiles - 1) * TM, TM)
        for h in range(n_heads):
            ahead = h + STAGE_GAP
            scores(row0 if ahead < n_heads else next0, ahead % n_heads, ahead % n_heads)
            softmax(h)
            behind = h - STAGE_GAP
            finish(row0 if behind >= 0 else prev0, behind % n_heads)
        return carry

    lax.fori_loop(0, n_tiles, tile_body, 0)
    for h in range(n_heads - STAGE_GAP, n_heads):
        finish(n_tok - TM, h)
    if not with_ctx:
        o_ctx[...] = jnp.zeros(o_ctx.shape, BF16)


_HEADS_A = tuple((h // 2, "head0" if h % 2 == 0 else "head1", 0, 2 * (h // 2), 0, 2 * (h // 2) + 1)
                 for h in range(4))
_HEADS_B = tuple((h, None, 0, h, 1, h) for h in range(4))


def _batch_specs(geo, width):
    ctx0 = geo.n_lat // CTX_LEN
    return [pl.BlockSpec((geo.n_tok, width), lambda b: (b, 0)),
            pl.BlockSpec((CTX_LEN, width), lambda b: (ctx0 + b, 0))]


def _attn_out(geo):
    specs = [pl.BlockSpec((geo.n_tok, BRANCH_WIDTH), lambda b: (b, 0)),
             pl.BlockSpec((CTX_LEN, BRANCH_WIDTH), lambda b: (b, 0))]
    shapes = [jax.ShapeDtypeStruct((geo.n_lat, BRANCH_WIDTH), BF16),
              jax.ShapeDtypeStruct((geo.bsz * CTX_LEN, BRANCH_WIDTH), BF16)]
    return specs, shapes


def _dense_attn_call(name, geo, q, kvs, heads, with_ctx):
    s_len = geo.n_tok + CTX_LEN
    in_specs = _batch_specs(geo, q.shape[1])
    args = [q, q]
    for kv in kvs:
        in_specs += _batch_specs(geo, kv.shape[1])
        args += [kv, kv]
    out_specs, out_shape = _attn_out(geo)
    return pl.pallas_call(
        functools.partial(_dense_attn_kernel, heads=heads, n_kv=len(kvs), with_ctx=with_ctx),
        grid=(geo.bsz,),
        in_specs=in_specs,
        out_specs=out_specs,
        out_shape=out_shape,
        scratch_shapes=[pltpu.VMEM((TM, s_len), F32)] * len(heads) + [pltpu.VMEM((TM, s_len), BF16)] * len(heads)
                       + [pltpu.VMEM((TM, LANES), F32)] * (len(heads) // 2),
        compiler_params=_params(),
        name=name,
    )(*args)


def _window_attn_kernel(sink_ref, q_lat, q_ctx, kv_lat, kv_ctx, o_lat, o_ctx, *scratch, with_ctx, layer):
    s_bufs, p_bufs, e_bufs, bias_ref = scratch[0:4], scratch[4:8], scratch[8:12], scratch[12]
    n_tok = q_lat.shape[0]
    n_tiles = n_tok // TM
    assert n_tiles >= 2

    row = lax.broadcasted_iota(jnp.int32, (2 * TM, BAND), 0)
    col = lax.broadcasted_iota(jnp.int32, (2 * TM, BAND), 1)
    rel0 = col - jnp.where(row < TM, row, row - TM)
    for place, off in enumerate((0, -WINDOW, TM - BAND)):
        bias_ref[place] = jnp.where(jnp.abs(rel0 + off) <= WINDOW, 0.0, -1e30)

    def sink_col(j):
        row = lax.broadcasted_iota(jnp.int32, (2 * TM, 1), 0)
        return jnp.where(row < TM, sink_ref[layer, 2 * j], sink_ref[layer, 2 * j + 1]) * LOG2E

    def cols(j):
        return slice(2 * LANES * j, 2 * LANES * j + LANES), slice(2 * LANES * j + LANES, 2 * LANES * (j + 1))

    def store_pair(ref, row0, j, r, e):
        ref[pl.ds(row0, TM), LANES * j:LANES * (j + 1)] = _normalise_pair(r[:TM], r[TM:], e[:TM], e[TM:]).astype(BF16)

    if with_ctx:
        for j in range(2):
            c_kk, c_v1 = cols(j)
            (p,), e = _probs([_dot_t(_stack_pair(q_ctx[:, LANES * j:LANES * (j + 1)]), kv_ctx[:, c_kk])], sink_col(j))
            store_pair(o_ctx, 0, j, _dot(p, kv_ctx[:, c_v1]), e)

    def band_start(i):
        if isinstance(i, int):
            return min(max(i * TM - WINDOW, 0), n_tok - BAND)
        return pl.multiple_of(jnp.clip(i * TM - WINDOW, 0, n_tok - BAND), WINDOW)

    def scores(i, j, buf):
        c_kk, _ = cols(j)
        row0 = i * TM if isinstance(i, int) else pl.multiple_of(i * TM, TM)
        qs = _stack_pair(q_lat[pl.ds(row0, TM), LANES * j:LANES * (j + 1)])
        s_bufs[buf][:, 0:CTX_LEN] = _dot_t(qs, kv_ctx[:, c_kk])
        if isinstance(i, int):
            place = 0 if i == 0 else 2 if i == n_tiles - 1 else 1
        else:
            place = jnp.where(i == 0, 0, jnp.where(i == n_tiles - 1, 2, 1))
        s_bufs[buf][:, CTX_LEN:] = _dot_t(qs, kv_lat[pl.ds(band_start(i), BAND), c_kk]) + bias_ref[place]

    def softmax(i, j, buf):
        e = _softmax_rows(s_bufs[buf], p_bufs[buf], sink_col(j))
        e_bufs[buf][...] = jnp.broadcast_to(e, e_bufs[buf].shape)

    def weighted(i, j, buf):
        _, c_v1 = cols(j)
        return (_dot(p_bufs[buf][:, 0:CTX_LEN], kv_ctx[:, c_v1])
                + _dot(p_bufs[buf][:, CTX_LEN:], kv_lat[pl.ds(band_start(i), BAND), c_v1]))

    def finish(i, j, buf):
        row0 = i * TM if isinstance(i, int) else pl.multiple_of(i * TM, TM)
        store_pair(o_lat, row0, j, weighted(i, j, buf), e_bufs[buf][...])

    def pair_of_tiles(i0, first, last):
        i1 = i0 + 1
        for c, (i, j) in enumerate(((i0, 0), (i0, 1), (i1, 0), (i1, 1))):
            if c < 2:
                scores(i1, j, c + 2)
            elif not last:
                scores(i0 + 2, j, c - 2)
            softmax(i, j, c)
            if c >= 2:
                finish(i0, j, c - 2)
            elif not first:
                finish(i0 - 1, j, c + 2)

    assert n_tiles % 2 == 0 and n_tiles >= 4
    scores(0, 0, 0)
    scores(0, 1, 1)
    pair_of_tiles(0, True, False)

    def body(k, carry):
        pair_of_tiles(2 * k, False, False)
        return carry

    lax.fori_loop(1, n_tiles // 2 - 1, body, 0)
    pair_of_tiles(n_tiles - 2, False, True)
    finish(n_tiles - 1, 0, 2)
    finish(n_tiles - 1, 1, 3)
    if not with_ctx:
        o_ctx[...] = jnp.zeros(o_ctx.shape, BF16)


def _window_attn_call(geo, l, sink, q, kv, with_ctx):
    n_keys = CTX_LEN + BAND
    out_specs, out_shape = _attn_out(geo)
    return pl.pallas_call(
        functools.partial(_window_attn_kernel, with_ctx=with_ctx, layer=l),
        grid=(geo.bsz,),
        in_specs=[pl.BlockSpec(memory_space=pltpu.SMEM)] + _batch_specs(geo, q.shape[1]) + _batch_specs(geo, kv.shape[1]),
        out_specs=out_specs,
        out_shape=out_shape,
        scratch_shapes=[pltpu.VMEM((2 * TM, n_keys), F32)] * 4 + [pltpu.VMEM((2 * TM, n_keys), BF16)] * 4
                       + [pltpu.VMEM((2 * TM, LANES), F32)] * 4 + [pltpu.VMEM((3, 2 * TM, BAND), F32)],
        compiler_params=_params(),
        name="attn_c",
    )(sink, q, q, kv, kv)


def _window_sums(ext):
    assert POOL_WINDOWS == (2, 4, 8, 16) and HALO >= 8
    n = ext.shape[0]
    back = lambda x, k: pltpu.roll(x, k, 0)
    ahead = lambda x, k: pltpu.roll(x, n - k, 0)
    e0, e1 = ext[:, 0:LANES], ext[:, LANES:2 * LANES]
    lo = _lane_lo(e0.shape, POOL_DIM)
    a2 = e0 + back(e0, 1)
    a4 = a2 + back(a2, 2)
    col0 = jnp.where(lo, a2, ahead(a4, 1))
    b2 = e1 + back(e1, 1)
    b4 = b2 + back(b2, 2)
    b8 = b4 + back(b4, 4)
    b16 = b8 + back(b8, 8)
    col1 = jnp.where(lo, ahead(b8, 3), ahead(b16, 7))
    return jnp.concatenate([col0[HALO:HALO + TM], col1[HALO:HALO + TM]], axis=1)


def _merge_kernel(xl_ref, xc_ref, mod_ref, oal_ref, oac_ref, obl_ref, obc_ref, ocl_ref, occ_ref,
                  u_ref, up_ref, un_ref, gpre_ref, wgl_ref, wpool_ref, dscale_ref,
                  wbr_ref, wout_ref, gpost_ref, out_ref, *, lat_blocks, per_b):
    d = D_MODEL
    i = pl.program_id(0)
    is_ctx = i >= lat_blocks
    blk = i % per_b
    lane = lax.broadcasted_iota(jnp.int32, (TM, 2 * LANES), 1)
    grp = lane // POOL_DIM
    row = lax.broadcasted_iota(jnp.int32, (TM, 2 * LANES), 0)
    p_lo = jnp.left_shift(1, grp)
    n_tok = jnp.where(is_ctx, CTX_LEN, per_b * RB)
    zero_halo = jnp.zeros((HALO, 2 * LANES), F32)

    for sub in range(N_SUB):
        rows = slice(TM * sub, TM * (sub + 1))
        x = jnp.where(is_ctx, _sub_rows(xc_ref, sub), _sub_rows(xl_ref, sub))
        hb = (_rms(x, gpre_ref[...]) * (1.0 + mod_ref[:, d:2 * d]) + mod_ref[:, 0:d]).astype(BF16)

        u = u_ref[rows, :]
        if sub == 0:
            prev = jnp.where(jnp.logical_or(is_ctx, blk == 0), zero_halo, up_ref[...])
        else:
            prev = jnp.where(is_ctx, zero_halo, u_ref[TM * sub - HALO:TM * sub, :])
        if sub == N_SUB - 1:
            nxt = jnp.where(jnp.logical_or(is_ctx, blk == per_b - 1), zero_halo, un_ref[...])
        else:
            nxt = jnp.where(is_ctx, zero_halo, u_ref[TM * (sub + 1):TM * (sub + 1) + HALO, :])
        sums = _window_sums(jnp.concatenate([prev, u, nxt], axis=0))
        pos = jnp.where(is_ctx, row, blk * RB + TM * sub + row)
        cnt = jnp.minimum(pos + p_lo, n_tok) - jnp.maximum(pos - p_lo, 0)
        diff = sums / cnt.astype(F32) - u
        o_d = _dot(diff.astype(BF16), wpool_ref[...]) * dscale_ref[...]

        branches = (jnp.where(is_ctx, oac_ref[rows, :], oal_ref[rows, :]),
                    jnp.where(is_ctx, obc_ref[rows, :], obl_ref[rows, :]),
                    jnp.where(is_ctx, occ_ref[rows, :], ocl_ref[rows, :]),
                    o_d.astype(BF16))
        acc = jnp.zeros((TM, d), F32)
        for k in range(N_BRANCH):
            gl = _dot(hb, wgl_ref[:, d * k:d * (k + 1)])
            proj = _dot(branches[k], wbr_ref[BRANCH_WIDTH * k:BRANCH_WIDTH * (k + 1), :])
            acc = acc + proj / (1.0 + jnp.exp(-gl))
        y = _dot(acc.astype(BF16), wout_ref[...])
        out_ref[rows, :] = x + mod_ref[:, 2 * d:3 * d] * _rms(y, gpost_ref[...])


def _merge_call(geo, l, n_blocks, x_lat, x_ctx, mods, o_pairs, u, wts):
    r = RB // HALO
    n_halo = geo.n_all // HALO
    up_spec = pl.BlockSpec((HALO, 256), lambda i: (jnp.maximum(i * r - 1, 0), 0))
    un_spec = pl.BlockSpec((HALO, 256), lambda i: (jnp.minimum((i + 1) * r, n_halo - 1), 0))
    o_specs, o_args = [], []
    for o_l, o_c in o_pairs:
        o_specs += [geo.lat_src(o_l), geo.ctx_src(o_c)]
        o_args += [o_l, o_c]
    consts = [wts[k] for k in ("mix_pre_g", "w_gl", "wpool", "d_scale", "w_branch", "w_out", "mix_post_g")]
    return pl.pallas_call(
        functools.partial(_merge_kernel, lat_blocks=geo.lat_blocks, per_b=geo.per_b),
        grid=(n_blocks,),
        in_specs=[geo.lat_src(x_lat), geo.ctx_src(x_ctx), geo.mod(l)] + o_specs
                 + [geo.rows(256), up_spec, un_spec] + [_resident(a, l) for a in consts],
        out_specs=geo.rows(D_MODEL),
        out_shape=jax.ShapeDtypeStruct((n_blocks * RB, D_MODEL), F32),
        compiler_params=_params(),
        name="merge",
    )(x_lat, x_ctx, mods, *o_args, u, u, u, *consts)


def _ffn_kernel(x_ref, mod_ref, gpre_ref, w1_ref, w3_ref, w2_ref, gpost_ref, out_ref):
    d = D_MODEL
    for sub in range(N_SUB):
        rows = slice(TM * sub, TM * (sub + 1))
        x = x_ref[rows, :]
        hb = (_rms(x, gpre_ref[...]) * (1.0 + mod_ref[:, 4 * d:5 * d]) + mod_ref[:, 3 * d:4 * d]).astype(BF16)
        a = _dot(hb, w1_ref[...])
        b = _dot(hb, w3_ref[...])
        act = (a / (1.0 + jnp.exp(-a)) * b).astype(BF16)
        f = _dot(act, w2_ref[...])
        out_ref[rows, :] = x + mod_ref[:, 5 * d:6 * d] * _rms(f, gpost_ref[...])


def _ffn_call(geo, l, x1, mods, wts):
    n, d = x1.shape
    consts = [wts[k] for k in ("ffn_pre_g", "w_ffn1", "w_ffn3", "w_ffn2", "ffn_post_g")]
    return pl.pallas_call(
        _ffn_kernel,
        grid=(n // RB,),
        in_specs=[geo.rows(d), geo.mod(l)] + [_resident(a, l) for a in consts],
        out_specs=geo.rows(d),
        out_shape=jax.ShapeDtypeStruct((n, d), F32),
        compiler_params=_params(),
        name="ffn",
    )(x1, mods, *consts)


def _rope_tables(n_tok):
    rows = n_tok // GRID_W
    row = jnp.repeat(jnp.arange(rows), GRID_W).astype(F32)
    col = jnp.tile(jnp.arange(GRID_W), rows).astype(F32)

    def cos_sin(rot_dim):
        n_freq = rot_dim // 4
        inv = ROPE_THETA ** (-jnp.arange(n_freq, dtype=F32) / n_freq)
        ang = jnp.concatenate([row[:, None] * inv, col[:, None] * inv], axis=-1)
        return jnp.cos(ang), jnp.sin(ang)

    c64, s64 = cos_sin(HEAD_DIM)
    cos_a = jnp.tile(c64, (1, 4))
    sin_a = jnp.concatenate([-s64, -s64, s64, s64], axis=-1)
    c32, s32 = cos_sin(B_ROPE)
    ones = lambda n: jnp.ones((n_tok, n), F32)
    zeros = lambda n: jnp.zeros((n_tok, n), F32)
    na, ra = B_NOPE // 2, B_ROPE // 2
    pad = LANES // 2 - na - ra
    cos_b = jnp.concatenate([ones(na), c32, ones(pad)] * 2, axis=-1)
    sin_b = jnp.concatenate([zeros(na), -s32, zeros(pad), zeros(na), s32, zeros(pad)], axis=-1)
    ident_c = jnp.ones((RB, LANES), F32)
    ident_s = jnp.zeros((RB, LANES), F32)
    return (jnp.concatenate([cos_a, ident_c]), jnp.concatenate([sin_a, ident_s]),
            jnp.concatenate([cos_b, ident_c]), jnp.concatenate([sin_b, ident_s]))


def _split_half_cols(w, c):
    half = HEAD_DIM // 2
    return [w[..., c:c + half], w[..., c + HEAD_DIM:c + HEAD_DIM + half],
            w[..., c + half:c + HEAD_DIM], w[..., c + HEAD_DIM + half:c + 2 * HEAD_DIM]]


def _nope_rope_cols(nope, rope, lead):
    na, ra = B_NOPE // 2, B_ROPE // 2
    zeros = lambda n: jnp.zeros(lead + (n,), F32)
    out = []
    for s in range(2):
        out += [zeros(na) if nope is None else nope[..., na * s:na * (s + 1)],
                zeros(ra) if rope is None else rope[..., ra * s:ra * (s + 1)],
                zeros(LANES // 2 - na - ra)]
    return out


def _w_in_kernel(w_ref, ws_ref, wgl_ref):
    w = w_ref[...]
    small = (_split_half_cols(w, 0) + _split_half_cols(w, 128) + _split_half_cols(w, 256) + [w[:, 384:896]]
             + _nope_rope_cols(None, w[:, 896:928], (w.shape[0],))
             + _split_half_cols(w, 928) + _split_half_cols(w, 1056) + _split_half_cols(w, 1184)
             + [w[:, 1312:1696]])
    ws_ref[...] = jnp.concatenate(small, axis=1).astype(BF16)
    wgl_ref[...] = w[:, 1696:].astype(BF16)


def _w_in_call(w_in):
    depth, d, d_in = w_in.shape
    n_gl = N_BRANCH * D_MODEL
    assert d_in == 1696 + n_gl
    rows = 128
    return pl.pallas_call(
        _w_in_kernel,
        grid=(depth, d // rows),
        in_specs=[pl.BlockSpec((None, rows, d_in), lambda l, r: (l, r, 0))],
        out_specs=[pl.BlockSpec((None, rows, N_SMALL), lambda l, r: (l, r, 0)),
                   pl.BlockSpec((None, rows, n_gl), lambda l, r: (l, r, 0))],
        out_shape=[jax.ShapeDtypeStruct((depth, d, N_SMALL), BF16), jax.ShapeDtypeStruct((depth, d, n_gl), BF16)],
        compiler_params=_params(2),
        name="w_in_layout",
    )(w_in)


def _stacked_weights(mix_pre_g, mix_post_g, ffn_pre_g, ffn_post_g, w_in, a_qn_g, a_kn_g, b_qa_g, b_kva_g,
                     b_w_qb, b_w_kvb, d_w_pool, d_scale, w_branch, w_out, w_ffn1, w_ffn3, w_ffn2):
    depth, d, _ = w_in.shape
    zeros = lambda rows, n: jnp.zeros((depth, rows, n), F32)
    half = HEAD_DIM // 2
    nope_rope = lambda nope, rope, rows: _nope_rope_cols(nope, rope, (depth, rows))
    w_small, w_gl = _w_in_call(w_in)
    per_q = B_NOPE + B_ROPE
    per_kv = B_NOPE + B_V
    qcols, kvcols = [], []
    for hd in range(B_HEADS):
        qcols += nope_rope(b_w_qb[:, :, per_q * hd:per_q * hd + B_NOPE],
                           b_w_qb[:, :, per_q * hd + B_NOPE:per_q * (hd + 1)], B_Q_RANK)
        kvcols += nope_rope(b_w_kvb[:, :, per_kv * hd:per_kv * hd + B_NOPE], None, B_KV_RANK)
        kvcols += [b_w_kvb[:, :, per_kv * hd + B_NOPE:per_kv * (hd + 1)], zeros(B_KV_RANK, LANES - B_V)]
    pool_rows = [jnp.concatenate([zeros(POOL_DIM, POOL_DIM * g), d_w_pool[:, g],
                                  zeros(POOL_DIM, POOL_DIM * (POOL_GROUPS - 1 - g))], axis=2)
                 for g in range(POOL_GROUPS)]
    gain = lambda g: g[:, None, :]
    head_gain = lambda g: jnp.concatenate([g[:, :half], g[:, :half], g[:, half:], g[:, half:]], axis=1)
    return {
        "w_small": w_small,
        "w_gl": w_gl,
        "wqb": jnp.concatenate(qcols, axis=2).astype(BF16),
        "wkvb": jnp.concatenate(kvcols, axis=2).astype(BF16),
        "wpool": jnp.concatenate(pool_rows, axis=1).astype(BF16),
        "w_branch": w_branch.reshape(depth, N_BRANCH * BRANCH_WIDTH, d).astype(BF16),
        "w_out": w_out.astype(BF16),
        "w_ffn1": w_ffn1.astype(BF16), "w_ffn3": w_ffn3.astype(BF16), "w_ffn2": w_ffn2.astype(BF16),
        "mix_pre_g": gain(mix_pre_g), "mix_post_g": gain(mix_post_g),
        "ffn_pre_g": gain(ffn_pre_g), "ffn_post_g": gain(ffn_post_g),
        "a_qn_g": gain(head_gain(a_qn_g)), "a_kn_g": gain(head_gain(a_kn_g)),
        "b_qa_g": gain(b_qa_g), "b_kva_g": gain(b_kva_g), "d_scale": gain(d_scale),
    }


def kernel(x, c, ctx, c_ctx, ada_w, ada_b, mix_pre_g, mix_post_g, ffn_pre_g, ffn_post_g, w_in, a_qn_g, a_kn_g,
           b_qa_g, b_kva_g, b_w_qb, b_w_kvb, c_sink, d_w_pool, d_scale, w_branch, w_out, w_ffn1, w_ffn3, w_ffn2):
    bsz, n_tok, d = x.shape
    depth = ada_w.shape[0]
    assert d == D_MODEL and ctx.shape[1] == CTX_LEN and bsz + 1 <= MOD_ROWS
    geo = _Rows(bsz, n_tok)

    cs = jnp.zeros((MOD_ROWS, d), F32).at[:bsz].set(c).at[bsz].set(c_ctx)
    mods = _mods_call(cs, ada_w, ada_b)[:, :, None, :]
    tabs = _rope_tables(n_tok)
    wts = _stacked_weights(mix_pre_g, mix_post_g, ffn_pre_g, ffn_post_g, w_in, a_qn_g, a_kn_g, b_qa_g, b_kva_g,
                           b_w_qb, b_w_kvb, d_w_pool, d_scale, w_branch, w_out, w_ffn1, w_ffn3, w_ffn2)

    x_lat, x_ctx = x, ctx
    for l in range(depth):
        with_ctx = l < depth - 1
        n_blocks = geo.all_blocks if with_ctx else geo.lat_blocks
        qa, kva, qb, kb, vb, qc, kvc, u = _pre_call(geo, l, x_lat, x_ctx, mods, wts, tabs)
        oa = _dense_attn_call("attn_a", geo, qa, [kva], _HEADS_A, with_ctx)
        ob = _dense_attn_call("attn_b", geo, qb, [kb, vb], _HEADS_B, with_ctx)
        oc = _window_attn_call(geo, l, c_sink, qc, kvc, with_ctx)
        x1 = _merge_call(geo, l, n_blocks, x_lat, x_ctx, mods, [oa, ob, oc], u, wts)
        x_lat = x_ctx = _ffn_call(geo, l, x1, mods, wts)
    return x_lat.reshape(bsz, n_tok, d)
```

```python
import functools

import jax
import jax.numpy as jnp
from jax import lax
from jax.experimental import pallas as pl
from jax.experimental.pallas import tpu as pltpu

D_MODEL = 1024
CTX_LEN = 256
GRID_W = 64
HEAD_DIM = 64
ROPE_THETA = 10000.0
EPS = 1e-6
N_BRANCH = 4
BRANCH_WIDTH = 256
B_HEADS = 4
B_Q_RANK = 256
B_KV_RANK = 128
B_NOPE = 64
B_ROPE = 32
B_V = 64
WINDOW = 128
POOL_WINDOWS = (2, 4, 8, 16)
POOL_GROUPS = 4
POOL_DIM = 64
LOG2E = 1.4426950408889634

LANES = 128
TM = CTX_LEN
N_SUB = 4
RB = N_SUB * TM
HALO = 8
BAND = 2 * TM
STAGE_GAP = 1
N_SMALL = 1792
MOD_ROWS = 16
VMEM_LIMIT = 56 * 1024 * 1024

F32 = jnp.float32
BF16 = jnp.bfloat16


def _dot(a, b):
    return jnp.dot(a, b, preferred_element_type=F32)


def _dot_t(a, b):
    return lax.dot_general(a, b, (((1,), (1,)), ((), ())), preferred_element_type=F32)


def _rms(x, g):
    return x * lax.rsqrt(jnp.mean(x * x, axis=-1, keepdims=True) + EPS) * g


def _lane_lo(shape, width):
    lane = lax.broadcasted_iota(jnp.int32, shape, 1)
    return (lane % (2 * width)) < width


def _head0(shape):
    return _lane_lo(shape, HEAD_DIM // 2)


def _rope(x, cos, sin):
    return x * cos + pltpu.roll(x, LANES // 2, 1) * sin


def _head_rms(x, g):
    h0 = _head0(x.shape)
    s = x * x
    s0 = jnp.sum(jnp.where(h0, s, 0.0), axis=-1, keepdims=True)
    s1 = jnp.sum(jnp.where(h0, 0.0, s), axis=-1, keepdims=True)
    ms = jnp.where(h0, s0, s1) * (1.0 / HEAD_DIM)
    return x * lax.rsqrt(ms + EPS) * g


def _pair_layouts(k, v):
    h0 = _head0(k.shape)
    lo = _lane_lo(v.shape, HEAD_DIM)
    k_up = pltpu.roll(k, HEAD_DIM // 2, 1)
    k_dn = pltpu.roll(k, LANES - HEAD_DIM // 2, 1)
    vr = pltpu.roll(v, HEAD_DIM, 1)
    return (jnp.where(h0, k, k_up), jnp.where(lo, v, 1.0), jnp.where(h0, k_dn, k), jnp.where(lo, vr, 1.0))


def _resident(arr, layer=None):
    if layer is None:
        nd = arr.ndim
        return pl.BlockSpec(arr.shape, lambda *_: (0,) * nd, pipeline_mode=pl.Buffered(1))
    nd = arr.ndim - 1
    return pl.BlockSpec((None,) + arr.shape[1:], lambda *_: (layer,) + (0,) * nd, pipeline_mode=pl.Buffered(1))


def _params(n_axes=1):
    return pltpu.CompilerParams(dimension_semantics=("arbitrary",) * n_axes, vmem_limit_bytes=VMEM_LIMIT)


class _Rows:
    def __init__(self, bsz, n_tok):
        assert n_tok % RB == 0 and (bsz * CTX_LEN) % RB == 0
        self.bsz, self.n_tok = bsz, n_tok
        self.n_lat = bsz * n_tok
        self.n_all = self.n_lat + bsz * CTX_LEN
        self.per_b = n_tok // RB
        self.lat_blocks = self.n_lat // RB
        self.all_blocks = self.n_all // RB

    def rows(self, width):
        return pl.BlockSpec((RB, width), lambda i: (i, 0))

    def lat_src(self, arr):
        last = self.lat_blocks - 1
        if arr.ndim == 3:
            return pl.BlockSpec((None, RB, arr.shape[2]),
                                lambda i: (jnp.minimum(i, last) // self.per_b, jnp.minimum(i, last) % self.per_b, 0))
        return pl.BlockSpec((RB, arr.shape[1]), lambda i: (jnp.minimum(i, last), 0))

    def ctx_src(self, arr):
        ctx_last = self.all_blocks - self.lat_blocks - 1
        rel = lambda i: jnp.clip(i - self.lat_blocks, 0, ctx_last)
        if arr.ndim == 3:
            return pl.BlockSpec((N_SUB, CTX_LEN, arr.shape[2]), lambda i: (rel(i), 0, 0))
        base = (arr.shape[0] - self.bsz * CTX_LEN) // RB
        return pl.BlockSpec((RB, arr.shape[1]), lambda i: (base + rel(i), 0))

    def mod(self, layer):
        return pl.BlockSpec((None, None, 1, 6 * D_MODEL),
                            lambda i: (layer, jnp.where(i < self.lat_blocks, i // self.per_b, self.bsz), 0, 0))

    def table(self):
        return pl.BlockSpec((RB, LANES), lambda i: (jnp.where(i < self.lat_blocks, i % self.per_b, self.per_b), 0))


def _mods_kernel(c_ref, w_ref, b_ref, o_ref):
    c = c_ref[...]
    s = c / (1.0 + jnp.exp(-c))
    o_ref[...] = _dot(s.astype(BF16), w_ref[...].astype(BF16)) + b_ref[...]


def _mods_call(cs, ada_w, ada_b):
    depth, d, n = ada_w.shape
    nb = 1536
    return pl.pallas_call(
        _mods_kernel,
        grid=(depth, n // nb),
        in_specs=[pl.BlockSpec((MOD_ROWS, d), lambda l, j: (0, 0)),
                  pl.BlockSpec((None, d, nb), lambda l, j: (l, 0, j)),
                  pl.BlockSpec((None, 1, nb), lambda l, j: (l, 0, j))],
        out_specs=pl.BlockSpec((None, MOD_ROWS, nb), lambda l, j: (l, 0, j)),
        out_shape=jax.ShapeDtypeStruct((depth, MOD_ROWS, n), F32),
        compiler_params=_params(2),
        name="adaln_mods",
    )(cs, ada_w, ada_b.reshape(depth, 1, n))


def _sub_rows(ref, sub):
    return ref[sub] if len(ref.shape) == 3 else ref[TM * sub:TM * (sub + 1), :]


def _pre_kernel(xl_ref, xc_ref, mod_ref, g_ref, w_ref, gq_ref, gk_ref, gqa_ref, gkva_ref, wqb_ref, wkvb_ref,
                cosa_ref, sina_ref, cosb_ref, sinb_ref,
                qa_ref, kva_ref, qb_ref, kb_ref, vb_ref, qc_ref, kvc_ref, u_ref, *, lat_blocks):
    d = D_MODEL
    is_ctx = pl.program_id(0) >= lat_blocks
    sc_a = HEAD_DIM ** -0.5 * LOG2E
    sc_b = (B_NOPE + B_ROPE) ** -0.5 * LOG2E
    for sub in range(N_SUB):
        rows = slice(TM * sub, TM * (sub + 1))
        x = jnp.where(is_ctx, _sub_rows(xc_ref, sub), _sub_rows(xl_ref, sub))
        h = _rms(x, g_ref[...]) * (1.0 + mod_ref[:, d:2 * d]) + mod_ref[:, 0:d]
        z = _dot_t(h.astype(BF16), w_ref[...])
        cosa, sina = cosa_ref[rows, :], sina_ref[rows, :]
        cosb, sinb = cosb_ref[rows, :], sinb_ref[rows, :]

        for j in range(2):
            q = _head_rms(z[:, LANES * j:LANES * (j + 1)], gq_ref[...])
            qa_ref[rows, LANES * j:LANES * (j + 1)] = (_rope(q, cosa, sina) * sc_a).astype(BF16)
        k = _rope(_head_rms(z[:, 256:384], gk_ref[...]), cosa, sina)
        for i, blk in enumerate(_pair_layouts(k, z[:, 384:512])):
            kva_ref[rows, LANES * i:LANES * (i + 1)] = blk.astype(BF16)

        qn = _rms(z[:, 512:768], gqa_ref[...]).astype(BF16)
        qb = _dot(qn, wqb_ref[...]) * sc_b
        kvn = _rms(z[:, 768:896], gkva_ref[...]).astype(BF16)
        kvb = _dot(kvn, wkvb_ref[...])
        kbr = _rope(z[:, 896:1024], cosb, sinb)
        lo = _lane_lo(kbr.shape, HEAD_DIM)
        for hd in range(B_HEADS):
            c0 = LANES * hd
            qb_ref[rows, c0:c0 + LANES] = _rope(qb[:, c0:c0 + LANES], cosb, sinb).astype(BF16)
            kb_ref[rows, c0:c0 + LANES] = (kvb[:, 2 * c0:2 * c0 + LANES] + kbr).astype(BF16)
            vb_ref[rows, c0:c0 + LANES] = jnp.where(lo, kvb[:, 2 * c0 + LANES:2 * c0 + 2 * LANES], 1.0).astype(BF16)

        for j in range(2):
            q = z[:, 1024 + LANES * j:1024 + LANES * (j + 1)]
            qc_ref[rows, LANES * j:LANES * (j + 1)] = (_rope(q, cosa, sina) * sc_a).astype(BF16)
        k = _rope(z[:, 1280:1408], cosa, sina)
        for i, blk in enumerate(_pair_layouts(k, z[:, 1408:1536])):
            kvc_ref[rows, LANES * i:LANES * (i + 1)] = blk.astype(BF16)

        u_ref[rows, :] = z[:, 1536:1792]


def _pre_call(geo, l, x_lat, x_ctx, mods, wts, tabs):
    widths = (256, 512, 512, 512, 512, 256, 512)
    out_shape = [jax.ShapeDtypeStruct((geo.n_all, w), BF16) for w in widths]
    out_shape.append(jax.ShapeDtypeStruct((geo.n_all, 256), F32))
    consts = [wts[k] for k in ("mix_pre_g", "w_small", "a_qn_g", "a_kn_g", "b_qa_g", "b_kva_g", "wqb", "wkvb")]
    return pl.pallas_call(
        functools.partial(_pre_kernel, lat_blocks=geo.lat_blocks),
        grid=(geo.all_blocks,),
        in_specs=[geo.lat_src(x_lat), geo.ctx_src(x_ctx), geo.mod(l)] + [_resident(a, l) for a in consts]
                 + [geo.table()] * 4,
        out_specs=[geo.rows(w) for w in widths] + [geo.rows(256)],
        out_shape=out_shape,
        compiler_params=_params(),
        name="pre_attn",
    )(x_lat, x_ctx, mods, *consts, *tabs)


def _probs(scores, extra=None):
    m = functools.reduce(jnp.maximum, [jnp.max(s, axis=-1, keepdims=True) for s in scores])
    if extra is not None:
        m = jnp.maximum(m, extra)
    ps = [jnp.exp2(s - m).astype(BF16) for s in scores]
    return ps, (None if extra is None else jnp.exp2(extra - m))


def _softmax_rows(s_ref, p_ref, extra=None):
    n = s_ref.shape[1]
    m = s_ref[:, 0:LANES]
    for c in range(1, n // LANES):
        m = jnp.maximum(m, s_ref[:, LANES * c:LANES * (c + 1)])
    m = jnp.max(m, axis=-1, keepdims=True)
    if extra is not None:
        m = jnp.maximum(m, extra)
    mb = jnp.broadcast_to(m, (s_ref.shape[0], LANES))
    for c in range(n // (2 * LANES)):
        cols = slice(2 * LANES * c, 2 * LANES * (c + 1))
        s = s_ref[:, cols]
        p_ref[:, cols] = jnp.concatenate([jnp.exp2(s[:, :LANES] - mb), jnp.exp2(s[:, LANES:] - mb)],
                                         axis=1).astype(BF16)
    return None if extra is None else jnp.exp2(extra - m)


def _normalise_pair(r0, r1, e0=None, e1=None):
    lo = _lane_lo(r0.shape, HEAD_DIM)
    r0r = pltpu.roll(r0, HEAD_DIM, 1)
    r1r = pltpu.roll(r1, HEAD_DIM, 1)
    den0 = r0r if e0 is None else r0r + e0
    den1 = r1 if e1 is None else r1 + e1
    return jnp.where(lo, r0 / den0, r1r / den1)


def _stack_pair(q):
    h0 = _head0(q.shape)
    zero = jnp.zeros_like(q)
    return jnp.concatenate([jnp.where(h0, q, zero), jnp.where(h0, zero, q)], axis=0)


def _dense_attn_kernel(*refs, heads, n_kv, with_ctx):
    q_lat, q_ctx = refs[0:2]
    kv_lat = refs[2:2 + 2 * n_kv:2]
    kv_ctx = refs[3:3 + 2 * n_kv:2]
    o_lat, o_ctx = refs[2 + 2 * n_kv:4 + 2 * n_kv]
    scratch = refs[4 + 2 * n_kv:]
    n_heads = len(heads)
    s_bufs, p_bufs, r_bufs = scratch[0:n_heads], scratch[n_heads:2 * n_heads], scratch[2 * n_heads:]
    n_tok = q_lat.shape[0]
    n_tiles = n_tok // TM

    def q_block(ref, row0, h):
        qcol, mask, *_ = heads[h]
        q = ref[pl.ds(row0, TM), LANES * qcol:LANES * (qcol + 1)]
        if mask is None:
            return q
        h0 = _head0(q.shape)
        return jnp.where(h0 if mask == "head0" else ~h0, q, jnp.zeros_like(q))

    def keys(refs_, h):
        _, _, karr, kcol, _, _ = heads[h]
        return refs_[karr][:, LANES * kcol:LANES * (kcol + 1)]

    def values(refs_, h):
        _, _, _, _, varr, vcol = heads[h]
        return refs_[varr][:, LANES * vcol:LANES * (vcol + 1)]

    def store_pair(ref, row0, j, r0, r1):
        ref[pl.ds(row0, TM), LANES * j:LANES * (j + 1)] = _normalise_pair(r0, r1).astype(BF16)

    if with_ctx:
        rs = []
        for h in range(n_heads):
            (p,), _ = _probs([_dot_t(q_block(q_ctx, 0, h), keys(kv_ctx, h))])
            rs.append(_dot(p, values(kv_ctx, h)))
        for j in range(n_heads // 2):
            store_pair(o_ctx, 0, j, rs[2 * j], rs[2 * j + 1])

    def scores(row0, h, buf):
        q = q_block(q_lat, row0, h)
        s_bufs[buf][:, 0:CTX_LEN] = _dot_t(q, keys(kv_ctx, h))
        s_bufs[buf][:, CTX_LEN:] = _dot_t(q, keys(kv_lat, h))

    def softmax(buf):
        _softmax_rows(s_bufs[buf], p_bufs[buf])

    def weighted(h, buf):
        return (_dot(p_bufs[buf][:, 0:CTX_LEN], values(kv_ctx, h))
                + _dot(p_bufs[buf][:, CTX_LEN:], values(kv_lat, h)))

    for h in range(n_heads - STAGE_GAP, n_heads):
        p_bufs[h][...] = jnp.zeros(p_bufs[h].shape, BF16)
    for rb in r_bufs:
        rb[...] = jnp.zeros(rb.shape, F32)
    for h in range(STAGE_GAP):
        scores(0, h, h)

    def finish(row0, hp):
        r = weighted(hp, hp)
        if hp % 2 == 0:
            r_bufs[hp // 2][...] = r
        else:
            store_pair(o_lat, row0, hp // 2, r_bufs[hp // 2][...], r)

    def tile_body(i, carry):
        row0 = pl.multiple_of(i * TM, TM)
        prev0 = pl.multiple_of(jnp.maximum(i - 1, 0) * TM, TM)
        next0 = pl.multiple_of(jnp.minimum(i + 1, n_tiles - 1) * TM, TM)
        for h in range(n_heads):
            ahead = h + STAGE_GAP
            scores(row0 if ahead < n_heads else next0, ahead % n_heads, ahead % n_heads)
            softmax(h)
            behind = h - STAGE_GAP
            finish(row0 if behind >= 0 else prev0, behind % n_heads)
        return carry

    lax.fori_loop(0, n_tiles, tile_body, 0)
    for h in range(n_heads - STAGE_GAP, n_heads):
        finish(n_tok - TM, h)
    if not with_ctx:
        o_ctx[...] = jnp.zeros(o_ctx.shape, BF16)


_HEADS_A = tuple((h // 2, "head0" if h % 2 == 0 else "head1", 0, 2 * (h // 2), 0, 2 * (h // 2) + 1)
                 for h in range(4))
_HEADS_B = tuple((h, None, 0, h, 1, h) for h in range(4))


def _batch_specs(geo, width):
    ctx0 = geo.n_lat // CTX_LEN
    return [pl.BlockSpec((geo.n_tok, width), lambda b: (b, 0)),
            pl.BlockSpec((CTX_LEN, width), lambda b: (ctx0 + b, 0))]


def _attn_out(geo):
    specs = [pl.BlockSpec((geo.n_tok, BRANCH_WIDTH), lambda b: (b, 0)),
             pl.BlockSpec((CTX_LEN, BRANCH_WIDTH), lambda b: (b, 0))]
    shapes = [jax.ShapeDtypeStruct((geo.n_lat, BRANCH_WIDTH), BF16),
              jax.ShapeDtypeStruct((geo.bsz * CTX_LEN, BRANCH_WIDTH), BF16)]
    return specs, shapes


def _dense_attn_call(name, geo, q, kvs, heads, with_ctx):
    s_len = geo.n_tok + CTX_LEN
    in_specs = _batch_specs(geo, q.shape[1])
    args = [q, q]
    for kv in kvs:
        in_specs += _batch_specs(geo, kv.shape[1])
        args += [kv, kv]
    out_specs, out_shape = _attn_out(geo)
    return pl.pallas_call(
        functools.partial(_dense_attn_kernel, heads=heads, n_kv=len(kvs), with_ctx=with_ctx),
        grid=(geo.bsz,),
        in_specs=in_specs,
        out_specs=out_specs,
        out_shape=out_shape,
        scratch_shapes=[pltpu.VMEM((TM, s_len), F32)] * len(heads) + [pltpu.VMEM((TM, s_len), BF16)] * len(heads)
                       + [pltpu.VMEM((TM, LANES), F32)] * (len(heads) // 2),
        compiler_params=_params(),
        name=name,
    )(*args)


def _window_attn_kernel(sink_ref, q_lat, q_ctx, kv_lat, kv_ctx, o_lat, o_ctx, *scratch, with_ctx, layer):
    s_bufs, p_bufs, e_bufs, bias_ref = scratch[0:4], scratch[4:8], scratch[8:12], scratch[12]
    n_tok = q_lat.shape[0]
    n_tiles = n_tok // TM
    assert n_tiles >= 2

    row = lax.broadcasted_iota(jnp.int32, (2 * TM, BAND), 0)
    col = lax.broadcasted_iota(jnp.int32, (2 * TM, BAND), 1)
    rel0 = col - jnp.where(row < TM, row, row - TM)
    for place, off in enumerate((0, -WINDOW, TM - BAND)):
        bias_ref[place] = jnp.where(jnp.abs(rel0 + off) <= WINDOW, 0.0, -1e30)

    def sink_col(j):
        row = lax.broadcasted_iota(jnp.int32, (2 * TM, 1), 0)
        return jnp.where(row < TM, sink_ref[layer, 2 * j], sink_ref[layer, 2 * j + 1]) * LOG2E

    def cols(j):
        return slice(2 * LANES * j, 2 * LANES * j + LANES), slice(2 * LANES * j + LANES, 2 * LANES * (j + 1))

    def store_pair(ref, row0, j, r, e):
        ref[pl.ds(row0, TM), LANES * j:LANES * (j + 1)] = _normalise_pair(r[:TM], r[TM:], e[:TM], e[TM:]).astype(BF16)

    if with_ctx:
        for j in range(2):
            c_kk, c_v1 = cols(j)
            (p,), e = _probs([_dot_t(_stack_pair(q_ctx[:, LANES * j:LANES * (j + 1)]), kv_ctx[:, c_kk])], sink_col(j))
            store_pair(o_ctx, 0, j, _dot(p, kv_ctx[:, c_v1]), e)

    def band_start(i):
        if isinstance(i, int):
            return min(max(i * TM - WINDOW, 0), n_tok - BAND)
        return pl.multiple_of(jnp.clip(i * TM - WINDOW, 0, n_tok - BAND), WINDOW)

    def scores(i, j, buf):
        c_kk, _ = cols(j)
        row0 = i * TM if isinstance(i, int) else pl.multiple_of(i * TM, TM)
        qs = _stack_pair(q_lat[pl.ds(row0, TM), LANES * j:LANES * (j + 1)])
        s_bufs[buf][:, 0:CTX_LEN] = _dot_t(qs, kv_ctx[:, c_kk])
        if isinstance(i, int):
            place = 0 if i == 0 else 2 if i == n_tiles - 1 else 1
        else:
            place = jnp.where(i == 0, 0, jnp.where(i == n_tiles - 1, 2, 1))
        s_bufs[buf][:, CTX_LEN:] = _dot_t(qs, kv_lat[pl.ds(band_start(i), BAND), c_kk]) + bias_ref[place]

    def softmax(i, j, buf):
        e = _softmax_rows(s_bufs[buf], p_bufs[buf], sink_col(j))
        e_bufs[buf][...] = jnp.broadcast_to(e, e_bufs[buf].shape)

    def weighted(i, j, buf):
        _, c_v1 = cols(j)
        return (_dot(p_bufs[buf][:, 0:CTX_LEN], kv_ctx[:, c_v1])
                + _dot(p_bufs[buf][:, CTX_LEN:], kv_lat[pl.ds(band_start(i), BAND), c_v1]))

    def finish(i, j, buf):
        row0 = i * TM if isinstance(i, int) else pl.multiple_of(i * TM, TM)
        store_pair(o_lat, row0, j, weighted(i, j, buf), e_bufs[buf][...])

    def pair_of_tiles(i0, first, last):
        i1 = i0 + 1
        for c, (i, j) in enumerate(((i0, 0), (i0, 1), (i1, 0), (i1, 1))):
            if c < 2:
                scores(i1, j, c + 2)
            elif not last:
                scores(i0 + 2, j, c - 2)
            softmax(i, j, c)
            if c >= 2:
                finish(i0, j, c - 2)
            elif not first:
                finish(i0 - 1, j, c + 2)

    assert n_tiles % 2 == 0 and n_tiles >= 4
    scores(0, 0, 0)
    scores(0, 1, 1)
    pair_of_tiles(0, True, False)

    def body(k, carry):
        pair_of_tiles(2 * k, False, False)
        return carry

    lax.fori_loop(1, n_tiles // 2 - 1, body, 0)
    pair_of_tiles(n_tiles - 2, False, True)
    finish(n_tiles - 1, 0, 2)
    finish(n_tiles - 1, 1, 3)
    if not with_ctx:
        o_ctx[...] = jnp.zeros(o_ctx.shape, BF16)


def _window_attn_call(geo, l, sink, q, kv, with_ctx):
    n_keys = CTX_LEN + BAND
    out_specs, out_shape = _attn_out(geo)
    return pl.pallas_call(
        functools.partial(_window_attn_kernel, with_ctx=with_ctx, layer=l),
        grid=(geo.bsz,),
        in_specs=[pl.BlockSpec(memory_space=pltpu.SMEM)] + _batch_specs(geo, q.shape[1]) + _batch_specs(geo, kv.shape[1]),
        out_specs=out_specs,
        out_shape=out_shape,
        scratch_shapes=[pltpu.VMEM((2 * TM, n_keys), F32)] * 4 + [pltpu.VMEM((2 * TM, n_keys), BF16)] * 4
                       + [pltpu.VMEM((2 * TM, LANES), F32)] * 4 + [pltpu.VMEM((3, 2 * TM, BAND), F32)],
        compiler_params=_params(),
        name="attn_c",
    )(sink, q, q, kv, kv)


def _window_sums(ext):
    assert POOL_WINDOWS == (2, 4, 8, 16) and HALO >= 8
    n = ext.shape[0]
    back = lambda x, k: pltpu.roll(x, k, 0)
    ahead = lambda x, k: pltpu.roll(x, n - k, 0)
    e0, e1 = ext[:, 0:LANES], ext[:, LANES:2 * LANES]
    lo = _lane_lo(e0.shape, POOL_DIM)
    a2 = e0 + back(e0, 1)
    a4 = a2 + back(a2, 2)
    col0 = jnp.where(lo, a2, ahead(a4, 1))
    b2 = e1 + back(e1, 1)
    b4 = b2 + back(b2, 2)
    b8 = b4 + back(b4, 4)
    b16 = b8 + back(b8, 8)
    col1 = jnp.where(lo, ahead(b8, 3), ahead(b16, 7))
    return jnp.concatenate([col0[HALO:HALO + TM], col1[HALO:HALO + TM]], axis=1)


def _merge_kernel(xl_ref, xc_ref, mod_ref, oal_ref, oac_ref, obl_ref, obc_ref, ocl_ref, occ_ref,
                  u_ref, up_ref, un_ref, gpre_ref, wgl_ref, wpool_ref, dscale_ref,
                  wbr_ref, wout_ref, gpost_ref, out_ref, *, lat_blocks, per_b):
    d = D_MODEL
    i = pl.program_id(0)
    is_ctx = i >= lat_blocks
    blk = i % per_b
    lane = lax.broadcasted_iota(jnp.int32, (TM, 2 * LANES), 1)
    grp = lane // POOL_DIM
    row = lax.broadcasted_iota(jnp.int32, (TM, 2 * LANES), 0)
    p_lo = jnp.left_shift(1, grp)
    n_tok = jnp.where(is_ctx, CTX_LEN, per_b * RB)
    zero_halo = jnp.zeros((HALO, 2 * LANES), F32)

    for sub in range(N_SUB):
        rows = slice(TM * sub, TM * (sub + 1))
        x = jnp.where(is_ctx, _sub_rows(xc_ref, sub), _sub_rows(xl_ref, sub))
        hb = (_rms(x, gpre_ref[...]) * (1.0 + mod_ref[:, d:2 * d]) + mod_ref[:, 0:d]).astype(BF16)

        u = u_ref[rows, :]
        if sub == 0:
            prev = jnp.where(jnp.logical_or(is_ctx, blk == 0), zero_halo, up_ref[...])
        else:
            prev = jnp.where(is_ctx, zero_halo, u_ref[TM * sub - HALO:TM * sub, :])
        if sub == N_SUB - 1:
            nxt = jnp.where(jnp.logical_or(is_ctx, blk == per_b - 1), zero_halo, un_ref[...])
        else:
            nxt = jnp.where(is_ctx, zero_halo, u_ref[TM * (sub + 1):TM * (sub + 1) + HALO, :])
        sums = _window_sums(jnp.concatenate([prev, u, nxt], axis=0))
        pos = jnp.where(is_ctx, row, blk * RB + TM * sub + row)
        cnt = jnp.minimum(pos + p_lo, n_tok) - jnp.maximum(pos - p_lo, 0)
        diff = sums / cnt.astype(F32) - u
        o_d = _dot(diff.astype(BF16), wpool_ref[...]) * dscale_ref[...]

        branches = (jnp.where(is_ctx, oac_ref[rows, :], oal_ref[rows, :]),
                    jnp.where(is_ctx, obc_ref[rows, :], obl_ref[rows, :]),
                    jnp.where(is_ctx, occ_ref[rows, :], ocl_ref[rows, :]),
                    o_d.astype(BF16))
        acc = jnp.zeros((TM, d), F32)
        for k in range(N_BRANCH):
            gl = _dot_t(hb, wgl_ref[d * k:d * (k + 1), :])
            proj = _dot(branches[k], wbr_ref[BRANCH_WIDTH * k:BRANCH_WIDTH * (k + 1), :])
            acc = acc + proj / (1.0 + jnp.exp(-gl))
        y = _dot(acc.astype(BF16), wout_ref[...])
        out_ref[rows, :] = x + mod_ref[:, 2 * d:3 * d] * _rms(y, gpost_ref[...])


def _merge_call(geo, l, n_blocks, x_lat, x_ctx, mods, o_pairs, u, wts):
    r = RB // HALO
    n_halo = geo.n_all // HALO
    up_spec = pl.BlockSpec((HALO, 256), lambda i: (jnp.maximum(i * r - 1, 0), 0))
    un_spec = pl.BlockSpec((HALO, 256), lambda i: (jnp.minimum((i + 1) * r, n_halo - 1), 0))
    o_specs, o_args = [], []
    for o_l, o_c in o_pairs:
        o_specs += [geo.lat_src(o_l), geo.ctx_src(o_c)]
        o_args += [o_l, o_c]
    consts = [wts[k] for k in ("mix_pre_g", "w_gl", "wpool", "d_scale", "w_branch", "w_out", "mix_post_g")]
    return pl.pallas_call(
        functools.partial(_merge_kernel, lat_blocks=geo.lat_blocks, per_b=geo.per_b),
        grid=(n_blocks,),
        in_specs=[geo.lat_src(x_lat), geo.ctx_src(x_ctx), geo.mod(l)] + o_specs
                 + [geo.rows(256), up_spec, un_spec] + [_resident(a, l) for a in consts],
        out_specs=geo.rows(D_MODEL),
        out_shape=jax.ShapeDtypeStruct((n_blocks * RB, D_MODEL), F32),
        compiler_params=_params(),
        name="merge",
    )(x_lat, x_ctx, mods, *o_args, u, u, u, *consts)


def _ffn_kernel(x_ref, mod_ref, gpre_ref, w1_ref, w3_ref, w2_ref, gpost_ref, out_ref):
    d = D_MODEL
    for sub in range(N_SUB):
        rows = slice(TM * sub, TM * (sub + 1))
        x = x_ref[rows, :]
        hb = (_rms(x, gpre_ref[...]) * (1.0 + mod_ref[:, 4 * d:5 * d]) + mod_ref[:, 3 * d:4 * d]).astype(BF16)
        a = _dot(hb, w1_ref[...])
        b = _dot(hb, w3_ref[...])
        act = (a / (1.0 + jnp.exp(-a)) * b).astype(BF16)
        f = _dot(act, w2_ref[...])
        out_ref[rows, :] = x + mod_ref[:, 5 * d:6 * d] * _rms(f, gpost_ref[...])


def _ffn_call(geo, l, x1, mods, wts):
    n, d = x1.shape
    consts = [wts[k] for k in ("ffn_pre_g", "w_ffn1", "w_ffn3", "w_ffn2", "ffn_post_g")]
    return pl.pallas_call(
        _ffn_kernel,
        grid=(n // RB,),
        in_specs=[geo.rows(d), geo.mod(l)] + [_resident(a, l) for a in consts],
        out_specs=geo.rows(d),
        out_shape=jax.ShapeDtypeStruct((n, d), F32),
        compiler_params=_params(),
        name="ffn",
    )(x1, mods, *consts)


def _rope_tables(n_tok):
    rows = n_tok // GRID_W
    row = jnp.repeat(jnp.arange(rows), GRID_W).astype(F32)
    col = jnp.tile(jnp.arange(GRID_W), rows).astype(F32)

    def cos_sin(rot_dim):
        n_freq = rot_dim // 4
        inv = ROPE_THETA ** (-jnp.arange(n_freq, dtype=F32) / n_freq)
        ang = jnp.concatenate([row[:, None] * inv, col[:, None] * inv], axis=-1)
        return jnp.cos(ang), jnp.sin(ang)

    c64, s64 = cos_sin(HEAD_DIM)
    cos_a = jnp.tile(c64, (1, 4))
    sin_a = jnp.concatenate([-s64, -s64, s64, s64], axis=-1)
    c32, s32 = cos_sin(B_ROPE)
    ones = lambda n: jnp.ones((n_tok, n), F32)
    zeros = lambda n: jnp.zeros((n_tok, n), F32)
    na, ra = B_NOPE // 2, B_ROPE // 2
    pad = LANES // 2 - na - ra
    cos_b = jnp.concatenate([ones(na), c32, ones(pad)] * 2, axis=-1)
    sin_b = jnp.concatenate([zeros(na), -s32, zeros(pad), zeros(na), s32, zeros(pad)], axis=-1)
    ident_c = jnp.ones((RB, LANES), F32)
    ident_s = jnp.zeros((RB, LANES), F32)
    return (jnp.concatenate([cos_a, ident_c]), jnp.concatenate([sin_a, ident_s]),
            jnp.concatenate([cos_b, ident_c]), jnp.concatenate([sin_b, ident_s]))


def _nope_rope_cols(nope, rope, lead):
    na, ra = B_NOPE // 2, B_ROPE // 2
    zeros = lambda n: jnp.zeros(lead + (n,), F32)
    out = []
    for s in range(2):
        out += [zeros(na) if nope is None else nope[..., na * s:na * (s + 1)],
                zeros(ra) if rope is None else rope[..., ra * s:ra * (s + 1)],
                zeros(LANES // 2 - na - ra)]
    return out


def _w_in_layouts(w_in):
    w_t = jnp.swapaxes(w_in, 1, 2)
    depth, _, d = w_t.shape
    half, na, ra = HEAD_DIM // 2, B_NOPE // 2, B_ROPE // 2
    zeros = lambda n: jnp.zeros((depth, n, d), F32)

    def split_half(c):
        return [w_t[:, c:c + half], w_t[:, c + HEAD_DIM:c + HEAD_DIM + half],
                w_t[:, c + half:c + HEAD_DIM], w_t[:, c + HEAD_DIM + half:c + 2 * HEAD_DIM]]

    kbr = w_t[:, 896:928]
    pad = LANES // 2 - na - ra
    rope_key = [zeros(na), kbr[:, :ra], zeros(pad), zeros(na), kbr[:, ra:], zeros(pad)]
    small = (split_half(0) + split_half(128) + split_half(256) + [w_t[:, 384:896]] + rope_key
             + split_half(928) + split_half(1056) + split_half(1184) + [w_t[:, 1312:1696]])
    return jnp.concatenate(small, axis=1).astype(BF16), w_t[:, 1696:].astype(BF16)


def _stacked_weights(mix_pre_g, mix_post_g, ffn_pre_g, ffn_post_g, w_in, a_qn_g, a_kn_g, b_qa_g, b_kva_g,
                     b_w_qb, b_w_kvb, d_w_pool, d_scale, w_branch, w_out, w_ffn1, w_ffn3, w_ffn2):
    depth, d, _ = w_in.shape
    zeros = lambda rows, n: jnp.zeros((depth, rows, n), F32)
    half = HEAD_DIM // 2
    nope_rope = lambda nope, rope, rows: _nope_rope_cols(nope, rope, (depth, rows))
    w_small, w_gl = _w_in_layouts(w_in)
    per_q = B_NOPE + B_ROPE
    per_kv = B_NOPE + B_V
    qcols, kvcols = [], []
    for hd in range(B_HEADS):
        qcols += nope_rope(b_w_qb[:, :, per_q * hd:per_q * hd + B_NOPE],
                           b_w_qb[:, :, per_q * hd + B_NOPE:per_q * (hd + 1)], B_Q_RANK)
        kvcols += nope_rope(b_w_kvb[:, :, per_kv * hd:per_kv * hd + B_NOPE], None, B_KV_RANK)
        kvcols += [b_w_kvb[:, :, per_kv * hd + B_NOPE:per_kv * (hd + 1)], zeros(B_KV_RANK, LANES - B_V)]
    pool_rows = [jnp.concatenate([zeros(POOL_DIM, POOL_DIM * g), d_w_pool[:, g],
                                  zeros(POOL_DIM, POOL_DIM * (POOL_GROUPS - 1 - g))], axis=2)
                 for g in range(POOL_GROUPS)]
    gain = lambda g: g[:, None, :]
    head_gain = lambda g: jnp.concatenate([g[:, :half], g[:, :half], g[:, half:], g[:, half:]], axis=1)
    return {
        "w_small": w_small,
        "w_gl": w_gl,
        "wqb": jnp.concatenate(qcols, axis=2).astype(BF16),
        "wkvb": jnp.concatenate(kvcols, axis=2).astype(BF16),
        "wpool": jnp.concatenate(pool_rows, axis=1).astype(BF16),
        "w_branch": w_branch.reshape(depth, N_BRANCH * BRANCH_WIDTH, d).astype(BF16),
        "w_out": w_out.astype(BF16),
        "w_ffn1": w_ffn1.astype(BF16), "w_ffn3": w_ffn3.astype(BF16), "w_ffn2": w_ffn2.astype(BF16),
        "mix_pre_g": gain(mix_pre_g), "mix_post_g": gain(mix_post_g),
        "ffn_pre_g": gain(ffn_pre_g), "ffn_post_g": gain(ffn_post_g),
        "a_qn_g": gain(head_gain(a_qn_g)), "a_kn_g": gain(head_gain(a_kn_g)),
        "b_qa_g": gain(b_qa_g), "b_kva_g": gain(b_kva_g), "d_scale": gain(d_scale),
    }


def kernel(x, c, ctx, c_ctx, ada_w, ada_b, mix_pre_g, mix_post_g, ffn_pre_g, ffn_post_g, w_in, a_qn_g, a_kn_g,
           b_qa_g, b_kva_g, b_w_qb, b_w_kvb, c_sink, d_w_pool, d_scale, w_branch, w_out, w_ffn1, w_ffn3, w_ffn2):
    bsz, n_tok, d = x.shape
    depth = ada_w.shape[0]
    assert d == D_MODEL and ctx.shape[1] == CTX_LEN and bsz + 1 <= MOD_ROWS
    geo = _Rows(bsz, n_tok)

    cs = jnp.zeros((MOD_ROWS, d), F32).at[:bsz].set(c).at[bsz].set(c_ctx)
    mods = _mods_call(cs, ada_w, ada_b)[:, :, None, :]
    tabs = _rope_tables(n_tok)
    wts = _stacked_weights(mix_pre_g, mix_post_g, ffn_pre_g, ffn_post_g, w_in, a_qn_g, a_kn_g, b_qa_g, b_kva_g,
                           b_w_qb, b_w_kvb, d_w_pool, d_scale, w_branch, w_out, w_ffn1, w_ffn3, w_ffn2)

    x_lat, x_ctx = x, ctx
    for l in range(depth):
        with_ctx = l < depth - 1
        n_blocks = geo.all_blocks if with_ctx else geo.lat_blocks
        qa, kva, qb, kb, vb, qc, kvc, u = _pre_call(geo, l, x_lat, x_ctx, mods, wts, tabs)
        oa = _dense_attn_call("attn_a", geo, qa, [kva], _HEADS_A, with_ctx)
        ob = _dense_attn_call("attn_b", geo, qb, [kb, vb], _HEADS_B, with_ctx)
        oc = _window_attn_call(geo, l, c_sink, qc, kvc, with_ctx)
        x1 = _merge_call(geo, l, n_blocks, x_lat, x_ctx, mods, [oa, ob, oc], u, wts)
        x_lat = x_ctx = _ffn_call(geo, l, x1, mods, wts)
    return x_lat.reshape(bsz, n_tok, d)
```

```python
import functools

import jax
import jax.numpy as jnp
from jax import lax
from jax.experimental import pallas as pl
from jax.experimental.pallas import tpu as pltpu

D_MODEL = 1024
CTX_LEN = 256
GRID_W = 64
HEAD_DIM = 64
ROPE_THETA = 10000.0
EPS = 1e-6
N_BRANCH = 4
BRANCH_WIDTH = 256
B_HEADS = 4
B_Q_RANK = 256
B_KV_RANK = 128
B_NOPE = 64
B_ROPE = 32
B_V = 64
WINDOW = 128
POOL_WINDOWS = (2, 4, 8, 16)
POOL_GROUPS = 4
POOL_DIM = 64
LOG2E = 1.4426950408889634

LANES = 128
TM = CTX_LEN
N_SUB = 4
RB = N_SUB * TM
HALO = 8
BAND = 2 * TM
STAGE_GAP = 1
N_SMALL = 1792
MOD_ROWS = 16
VMEM_LIMIT = 56 * 1024 * 1024

F32 = jnp.float32
BF16 = jnp.bfloat16


def _dot(a, b):
    return jnp.dot(a, b, preferred_element_type=F32)


def _dot_t(a, b):
    return lax.dot_general(a, b, (((1,), (1,)), ((), ())), preferred_element_type=F32)


def _rms(x, g):
    return x * lax.rsqrt(jnp.mean(x * x, axis=-1, keepdims=True) + EPS) * g


def _lane_lo(shape, width):
    lane = lax.broadcasted_iota(jnp.int32, shape, 1)
    return (lane % (2 * width)) < width


def _head0(shape):
    return _lane_lo(shape, HEAD_DIM // 2)


def _rope(x, cos, sin):
    return x * cos + pltpu.roll(x, LANES // 2, 1) * sin


def _head_rms(x, g):
    h0 = _head0(x.shape)
    s = x * x
    s0 = jnp.sum(jnp.where(h0, s, 0.0), axis=-1, keepdims=True)
    s1 = jnp.sum(jnp.where(h0, 0.0, s), axis=-1, keepdims=True)
    ms = jnp.where(h0, s0, s1) * (1.0 / HEAD_DIM)
    return x * lax.rsqrt(ms + EPS) * g


def _pair_layouts(k, v):
    h0 = _head0(k.shape)
    lo = _lane_lo(v.shape, HEAD_DIM)
    k_up = pltpu.roll(k, HEAD_DIM // 2, 1)
    k_dn = pltpu.roll(k, LANES - HEAD_DIM // 2, 1)
    vr = pltpu.roll(v, HEAD_DIM, 1)
    return (jnp.where(h0, k, k_up), jnp.where(lo, v, 1.0), jnp.where(h0, k_dn, k), jnp.where(lo, vr, 1.0))


def _resident(arr, layer=None):
    if layer is None:
        nd = arr.ndim
        return pl.BlockSpec(arr.shape, lambda *_: (0,) * nd, pipeline_mode=pl.Buffered(1))
    nd = arr.ndim - 1
    return pl.BlockSpec((None,) + arr.shape[1:], lambda *_: (layer,) + (0,) * nd, pipeline_mode=pl.Buffered(1))


def _params(n_axes=1):
    return pltpu.CompilerParams(dimension_semantics=("arbitrary",) * n_axes, vmem_limit_bytes=VMEM_LIMIT)


class _Rows:
    def __init__(self, bsz, n_tok):
        assert n_tok % RB == 0 and (bsz * CTX_LEN) % RB == 0
        self.bsz, self.n_tok = bsz, n_tok
        self.n_lat = bsz * n_tok
        self.n_all = self.n_lat + bsz * CTX_LEN
        self.per_b = n_tok // RB
        self.lat_blocks = self.n_lat // RB
        self.all_blocks = self.n_all // RB

    def rows(self, width):
        return pl.BlockSpec((RB, width), lambda i: (i, 0))

    def lat_src(self, arr):
        last = self.lat_blocks - 1
        if arr.ndim == 3:
            return pl.BlockSpec((None, RB, arr.shape[2]),
                                lambda i: (jnp.minimum(i, last) // self.per_b, jnp.minimum(i, last) % self.per_b, 0))
        return pl.BlockSpec((RB, arr.shape[1]), lambda i: (jnp.minimum(i, last), 0))

    def ctx_src(self, arr):
        ctx_last = self.all_blocks - self.lat_blocks - 1
        rel = lambda i: jnp.clip(i - self.lat_blocks, 0, ctx_last)
        if arr.ndim == 3:
            return pl.BlockSpec((N_SUB, CTX_LEN, arr.shape[2]), lambda i: (rel(i), 0, 0))
        base = (arr.shape[0] - self.bsz * CTX_LEN) // RB
        return pl.BlockSpec((RB, arr.shape[1]), lambda i: (base + rel(i), 0))

    def mod(self, layer):
        return pl.BlockSpec((None, None, 1, 6 * D_MODEL),
                            lambda i: (layer, jnp.where(i < self.lat_blocks, i // self.per_b, self.bsz), 0, 0))

    def table(self):
        return pl.BlockSpec((RB, LANES), lambda i: (jnp.where(i < self.lat_blocks, i % self.per_b, self.per_b), 0))


def _mods_kernel(c_ref, w_ref, b_ref, o_ref):
    c = c_ref[...]
    s = c / (1.0 + jnp.exp(-c))
    o_ref[...] = _dot(s.astype(BF16), w_ref[...].astype(BF16)) + b_ref[...]


def _mods_call(cs, ada_w, ada_b):
    depth, d, n = ada_w.shape
    nb = 1536
    return pl.pallas_call(
        _mods_kernel,
        grid=(depth, n // nb),
        in_specs=[pl.BlockSpec((MOD_ROWS, d), lambda l, j: (0, 0)),
                  pl.BlockSpec((None, d, nb), lambda l, j: (l, 0, j)),
                  pl.BlockSpec((None, 1, nb), lambda l, j: (l, 0, j))],
        out_specs=pl.BlockSpec((None, MOD_ROWS, nb), lambda l, j: (l, 0, j)),
        out_shape=jax.ShapeDtypeStruct((depth, MOD_ROWS, n), F32),
        compiler_params=_params(2),
        name="adaln_mods",
    )(cs, ada_w, ada_b.reshape(depth, 1, n))


def _sub_rows(ref, sub):
    return ref[sub] if len(ref.shape) == 3 else ref[TM * sub:TM * (sub + 1), :]


def _pre_kernel(xl_ref, xc_ref, mod_ref, g_ref, w_ref, gq_ref, gk_ref, gqa_ref, gkva_ref, wqb_ref, wkvb_ref,
                cosa_ref, sina_ref, cosb_ref, sinb_ref,
                qa_ref, kva_ref, qb_ref, kb_ref, vb_ref, qc_ref, kvc_ref, u_ref, *, lat_blocks):
    d = D_MODEL
    is_ctx = pl.program_id(0) >= lat_blocks
    sc_a = HEAD_DIM ** -0.5 * LOG2E
    sc_b = (B_NOPE + B_ROPE) ** -0.5 * LOG2E
    for sub in range(N_SUB):
        rows = slice(TM * sub, TM * (sub + 1))
        x = jnp.where(is_ctx, _sub_rows(xc_ref, sub), _sub_rows(xl_ref, sub))
        h = _rms(x, g_ref[...]) * (1.0 + mod_ref[:, d:2 * d]) + mod_ref[:, 0:d]
        z = _dot_t(h.astype(BF16), w_ref[...])
        cosa, sina = cosa_ref[rows, :], sina_ref[rows, :]
        cosb, sinb = cosb_ref[rows, :], sinb_ref[rows, :]

        for j in range(2):
            q = _head_rms(z[:, LANES * j:LANES * (j + 1)], gq_ref[...])
            qa_ref[rows, LANES * j:LANES * (j + 1)] = (_rope(q, cosa, sina) * sc_a).astype(BF16)
        k = _rope(_head_rms(z[:, 256:384], gk_ref[...]), cosa, sina)
        for i, blk in enumerate(_pair_layouts(k, z[:, 384:512])):
            kva_ref[rows, LANES * i:LANES * (i + 1)] = blk.astype(BF16)

        qn = _rms(z[:, 512:768], gqa_ref[...]).astype(BF16)
        qb = _dot(qn, wqb_ref[...]) * sc_b
        kvn = _rms(z[:, 768:896], gkva_ref[...]).astype(BF16)
        kvb = _dot(kvn, wkvb_ref[...])
        kbr = _rope(z[:, 896:1024], cosb, sinb)
        lo = _lane_lo(kbr.shape, HEAD_DIM)
        for hd in range(B_HEADS):
            c0 = LANES * hd
            qb_ref[rows, c0:c0 + LANES] = _rope(qb[:, c0:c0 + LANES], cosb, sinb).astype(BF16)
            kb_ref[rows, c0:c0 + LANES] = (kvb[:, 2 * c0:2 * c0 + LANES] + kbr).astype(BF16)
            vb_ref[rows, c0:c0 + LANES] = jnp.where(lo, kvb[:, 2 * c0 + LANES:2 * c0 + 2 * LANES], 1.0).astype(BF16)

        for j in range(2):
            q = z[:, 1024 + LANES * j:1024 + LANES * (j + 1)]
            qc_ref[rows, LANES * j:LANES * (j + 1)] = (_rope(q, cosa, sina) * sc_a).astype(BF16)
        k = _rope(z[:, 1280:1408], cosa, sina)
        for i, blk in enumerate(_pair_layouts(k, z[:, 1408:1536])):
            kvc_ref[rows, LANES * i:LANES * (i + 1)] = blk.astype(BF16)

        u_ref[rows, :] = z[:, 1536:1792]


def _pre_call(geo, l, x_lat, x_ctx, mods, wts, tabs):
    widths = (256, 512, 512, 512, 512, 256, 512)
    out_shape = [jax.ShapeDtypeStruct((geo.n_all, w), BF16) for w in widths]
    out_shape.append(jax.ShapeDtypeStruct((geo.n_all, 256), F32))
    consts = [wts[k] for k in ("mix_pre_g", "w_small", "a_qn_g", "a_kn_g", "b_qa_g", "b_kva_g", "wqb", "wkvb")]
    return pl.pallas_call(
        functools.partial(_pre_kernel, lat_blocks=geo.lat_blocks),
        grid=(geo.all_blocks,),
        in_specs=[geo.lat_src(x_lat), geo.ctx_src(x_ctx), geo.mod(l)] + [_resident(a, l) for a in consts]
                 + [geo.table()] * 4,
        out_specs=[geo.rows(w) for w in widths] + [geo.rows(256)],
        out_shape=out_shape,
        compiler_params=_params(),
        name="pre_attn",
    )(x_lat, x_ctx, mods, *consts, *tabs)


def _probs(scores, extra=None):
    m = functools.reduce(jnp.maximum, [jnp.max(s, axis=-1, keepdims=True) for s in scores])
    if extra is not None:
        m = jnp.maximum(m, extra)
    ps = [jnp.exp2(s - m).astype(BF16) for s in scores]
    return ps, (None if extra is None else jnp.exp2(extra - m))


def _softmax_rows(s_ref, p_ref, extra=None):
    n = s_ref.shape[1]
    m = s_ref[:, 0:LANES]
    for c in range(1, n // LANES):
        m = jnp.maximum(m, s_ref[:, LANES * c:LANES * (c + 1)])
    m = jnp.max(m, axis=-1, keepdims=True)
    if extra is not None:
        m = jnp.maximum(m, extra)
    mb = jnp.broadcast_to(m, (s_ref.shape[0], LANES))
    for c in range(n // (2 * LANES)):
        cols = slice(2 * LANES * c, 2 * LANES * (c + 1))
        s = s_ref[:, cols]
        p_ref[:, cols] = jnp.concatenate([jnp.exp2(s[:, :LANES] - mb), jnp.exp2(s[:, LANES:] - mb)],
                                         axis=1).astype(BF16)
    return None if extra is None else jnp.exp2(extra - m)


def _normalise_pair(r0, r1, e0=None, e1=None):
    lo = _lane_lo(r0.shape, HEAD_DIM)
    r0r = pltpu.roll(r0, HEAD_DIM, 1)
    r1r = pltpu.roll(r1, HEAD_DIM, 1)
    den0 = r0r if e0 is None else r0r + e0
    den1 = r1 if e1 is None else r1 + e1
    return jnp.where(lo, r0 / den0, r1r / den1)


def _stack_pair(q):
    h0 = _head0(q.shape)
    zero = jnp.zeros_like(q)
    return jnp.concatenate([jnp.where(h0, q, zero), jnp.where(h0, zero, q)], axis=0)


def _dense_attn_kernel(*refs, heads, n_kv, with_ctx):
    q_lat, q_ctx = refs[0:2]
    kv_lat = refs[2:2 + 2 * n_kv:2]
    kv_ctx = refs[3:3 + 2 * n_kv:2]
    o_lat, o_ctx = refs[2 + 2 * n_kv:4 + 2 * n_kv]
    scratch = refs[4 + 2 * n_kv:]
    n_heads = len(heads)
    s_bufs, p_bufs = scratch[0:n_heads], scratch[n_heads:2 * n_heads]
    r_bufs = scratch[2 * n_heads:2 * n_heads + n_heads // 2]
    kv_all = scratch[2 * n_heads + n_heads // 2:]
    n_tok = q_lat.shape[0]
    n_tiles = n_tok // TM

    for dst, ctx_src, lat_src in zip(kv_all, kv_ctx, kv_lat):
        dst[0:CTX_LEN, :] = ctx_src[...]
        dst[CTX_LEN:, :] = lat_src[...]

    def q_block(ref, row0, h):
        qcol, mask, *_ = heads[h]
        q = ref[pl.ds(row0, TM), LANES * qcol:LANES * (qcol + 1)]
        if mask is None:
            return q
        h0 = _head0(q.shape)
        return jnp.where(h0 if mask == "head0" else ~h0, q, jnp.zeros_like(q))

    def keys(refs_, h):
        _, _, karr, kcol, _, _ = heads[h]
        return refs_[karr][:, LANES * kcol:LANES * (kcol + 1)]

    def values(refs_, h):
        _, _, _, _, varr, vcol = heads[h]
        return refs_[varr][:, LANES * vcol:LANES * (vcol + 1)]

    def store_pair(ref, row0, j, r0, r1):
        ref[pl.ds(row0, TM), LANES * j:LANES * (j + 1)] = _normalise_pair(r0, r1).astype(BF16)

    if with_ctx:
        rs = []
        for h in range(n_heads):
            (p,), _ = _probs([_dot_t(q_block(q_ctx, 0, h), keys(kv_ctx, h))])
            rs.append(_dot(p, values(kv_ctx, h)))
        for j in range(n_heads // 2):
            store_pair(o_ctx, 0, j, rs[2 * j], rs[2 * j + 1])

    def scores(row0, h, buf):
        s_bufs[buf][...] = _dot_t(q_block(q_lat, row0, h), keys(kv_all, h))

    def softmax(buf):
        _softmax_rows(s_bufs[buf], p_bufs[buf])

    def weighted(h, buf):
        varr, vcol = heads[h][4], heads[h][5]
        base = vcol // 2 * 2
        r2 = _dot(p_bufs[buf][...], kv_all[varr][:, LANES * base:LANES * (base + 2)])
        return r2[:, LANES * (vcol - base):LANES * (vcol - base + 1)]

    for h in range(n_heads - STAGE_GAP, n_heads):
        p_bufs[h][...] = jnp.zeros(p_bufs[h].shape, BF16)
    for rb in r_bufs:
        rb[...] = jnp.zeros(rb.shape, F32)
    for h in range(STAGE_GAP):
        scores(0, h, h)

    def finish(row0, hp):
        r = weighted(hp, hp)
        if hp % 2 == 0:
            r_bufs[hp // 2][...] = r
        else:
            store_pair(o_lat, row0, hp // 2, r_bufs[hp // 2][...], r)

    def tile_body(i, carry):
        row0 = pl.multiple_of(i * TM, TM)
        prev0 = pl.multiple_of(jnp.maximum(i - 1, 0) * TM, TM)
        next0 = pl.multiple_of(jnp.minimum(i + 1, n_tiles - 1) * TM, TM)
        for h in range(n_heads):
            ahead = h + STAGE_GAP
            scores(row0 if ahead < n_heads else next0, ahead % n_heads, ahead % n_heads)
            softmax(h)
            behind = h - STAGE_GAP
            finish(row0 if behind >= 0 else prev0, behind % n_heads)
        return carry

    lax.fori_loop(0, n_tiles, tile_body, 0)
    for h in range(n_heads - STAGE_GAP, n_heads):
        finish(n_tok - TM, h)
    if not with_ctx:
        o_ctx[...] = jnp.zeros(o_ctx.shape, BF16)


_HEADS_A = tuple((h // 2, "head0" if h % 2 == 0 else "head1", 0, 2 * (h // 2), 0, 2 * (h // 2) + 1)
                 for h in range(4))
_HEADS_B = tuple((h, None, 0, h, 1, h) for h in range(4))


def _batch_specs(geo, width):
    ctx0 = geo.n_lat // CTX_LEN
    return [pl.BlockSpec((geo.n_tok, width), lambda b: (b, 0)),
            pl.BlockSpec((CTX_LEN, width), lambda b: (ctx0 + b, 0))]


def _attn_out(geo):
    specs = [pl.BlockSpec((geo.n_tok, BRANCH_WIDTH), lambda b: (b, 0)),
             pl.BlockSpec((CTX_LEN, BRANCH_WIDTH), lambda b: (b, 0))]
    shapes = [jax.ShapeDtypeStruct((geo.n_lat, BRANCH_WIDTH), BF16),
              jax.ShapeDtypeStruct((geo.bsz * CTX_LEN, BRANCH_WIDTH), BF16)]
    return specs, shapes


def _dense_attn_call(name, geo, q, kvs, heads, with_ctx):
    s_len = geo.n_tok + CTX_LEN
    in_specs = _batch_specs(geo, q.shape[1])
    args = [q, q]
    for kv in kvs:
        in_specs += _batch_specs(geo, kv.shape[1])
        args += [kv, kv]
    out_specs, out_shape = _attn_out(geo)
    return pl.pallas_call(
        functools.partial(_dense_attn_kernel, heads=heads, n_kv=len(kvs), with_ctx=with_ctx),
        grid=(geo.bsz,),
        in_specs=in_specs,
        out_specs=out_specs,
        out_shape=out_shape,
        scratch_shapes=[pltpu.VMEM((TM, s_len), F32)] * len(heads) + [pltpu.VMEM((TM, s_len), BF16)] * len(heads)
                       + [pltpu.VMEM((TM, LANES), F32)] * (len(heads) // 2)
                       + [pltpu.VMEM((s_len, kv.shape[1]), BF16) for kv in kvs],
        compiler_params=_params(),
        name=name,
    )(*args)


def _window_attn_kernel(sink_ref, q_lat, q_ctx, kv_lat, kv_ctx, o_lat, o_ctx, *scratch, with_ctx, layer):
    s_bufs, p_bufs, e_bufs, bias_ref = scratch[0:4], scratch[4:8], scratch[8:12], scratch[12]
    n_tok = q_lat.shape[0]
    n_tiles = n_tok // TM
    assert n_tiles >= 2

    row = lax.broadcasted_iota(jnp.int32, (2 * TM, BAND), 0)
    col = lax.broadcasted_iota(jnp.int32, (2 * TM, BAND), 1)
    rel0 = col - jnp.where(row < TM, row, row - TM)
    for place, off in enumerate((0, -WINDOW, TM - BAND)):
        bias_ref[place] = jnp.where(jnp.abs(rel0 + off) <= WINDOW, 0.0, -1e30)

    def sink_col(j):
        row = lax.broadcasted_iota(jnp.int32, (2 * TM, 1), 0)
        return jnp.where(row < TM, sink_ref[layer, 2 * j], sink_ref[layer, 2 * j + 1]) * LOG2E

    def cols(j):
        return slice(2 * LANES * j, 2 * LANES * j + LANES), slice(2 * LANES * j + LANES, 2 * LANES * (j + 1))

    def store_pair(ref, row0, j, r, e):
        ref[pl.ds(row0, TM), LANES * j:LANES * (j + 1)] = _normalise_pair(r[:TM], r[TM:], e[:TM], e[TM:]).astype(BF16)

    if with_ctx:
        for j in range(2):
            c_kk, c_v1 = cols(j)
            (p,), e = _probs([_dot_t(_stack_pair(q_ctx[:, LANES * j:LANES * (j + 1)]), kv_ctx[:, c_kk])], sink_col(j))
            store_pair(o_ctx, 0, j, _dot(p, kv_ctx[:, c_v1]), e)

    def band_start(i):
        if isinstance(i, int):
            return min(max(i * TM - WINDOW, 0), n_tok - BAND)
        return pl.multiple_of(jnp.clip(i * TM - WINDOW, 0, n_tok - BAND), WINDOW)

    def scores(i, j, buf):
        c_kk, _ = cols(j)
        row0 = i * TM if isinstance(i, int) else pl.multiple_of(i * TM, TM)
        qs = _stack_pair(q_lat[pl.ds(row0, TM), LANES * j:LANES * (j + 1)])
        s_bufs[buf][:, 0:CTX_LEN] = _dot_t(qs, kv_ctx[:, c_kk])
        if isinstance(i, int):
            place = 0 if i == 0 else 2 if i == n_tiles - 1 else 1
        else:
            place = jnp.where(i == 0, 0, jnp.where(i == n_tiles - 1, 2, 1))
        s_bufs[buf][:, CTX_LEN:] = _dot_t(qs, kv_lat[pl.ds(band_start(i), BAND), c_kk]) + bias_ref[place]

    def softmax(i, j, buf):
        e = _softmax_rows(s_bufs[buf], p_bufs[buf], sink_col(j))
        e_bufs[buf][...] = jnp.broadcast_to(e, e_bufs[buf].shape)

    def weighted(i, j, buf):
        both = slice(2 * LANES * j, 2 * LANES * (j + 1))
        r2 = (_dot(p_bufs[buf][:, 0:CTX_LEN], kv_ctx[:, both])
              + _dot(p_bufs[buf][:, CTX_LEN:], kv_lat[pl.ds(band_start(i), BAND), both]))
        return r2[:, LANES:]

    def finish(i, j, buf):
        row0 = i * TM if isinstance(i, int) else pl.multiple_of(i * TM, TM)
        store_pair(o_lat, row0, j, weighted(i, j, buf), e_bufs[buf][...])

    def pair_of_tiles(i0, first, last):
        i1 = i0 + 1
        for c, (i, j) in enumerate(((i0, 0), (i0, 1), (i1, 0), (i1, 1))):
            if c < 2:
                scores(i1, j, c + 2)
            elif not last:
                scores(i0 + 2, j, c - 2)
            softmax(i, j, c)
            if c >= 2:
                finish(i0, j, c - 2)
            elif not first:
                finish(i0 - 1, j, c + 2)

    assert n_tiles % 2 == 0 and n_tiles >= 4
    scores(0, 0, 0)
    scores(0, 1, 1)
    pair_of_tiles(0, True, False)

    def body(k, carry):
        pair_of_tiles(2 * k, False, False)
        return carry

    lax.fori_loop(1, n_tiles // 2 - 1, body, 0)
    pair_of_tiles(n_tiles - 2, False, True)
    finish(n_tiles - 1, 0, 2)
    finish(n_tiles - 1, 1, 3)
    if not with_ctx:
        o_ctx[...] = jnp.zeros(o_ctx.shape, BF16)


def _window_attn_call(geo, l, sink, q, kv, with_ctx):
    n_keys = CTX_LEN + BAND
    out_specs, out_shape = _attn_out(geo)
    return pl.pallas_call(
        functools.partial(_window_attn_kernel, with_ctx=with_ctx, layer=l),
        grid=(geo.bsz,),
        in_specs=[pl.BlockSpec(memory_space=pltpu.SMEM)] + _batch_specs(geo, q.shape[1]) + _batch_specs(geo, kv.shape[1]),
        out_specs=out_specs,
        out_shape=out_shape,
        scratch_shapes=[pltpu.VMEM((2 * TM, n_keys), F32)] * 4 + [pltpu.VMEM((2 * TM, n_keys), BF16)] * 4
                       + [pltpu.VMEM((2 * TM, LANES), F32)] * 4 + [pltpu.VMEM((3, 2 * TM, BAND), F32)],
        compiler_params=_params(),
        name="attn_c",
    )(sink, q, q, kv, kv)


def _window_sums(ext):
    assert POOL_WINDOWS == (2, 4, 8, 16) and HALO >= 8
    n = ext.shape[0]
    back = lambda x, k: pltpu.roll(x, k, 0)
    ahead = lambda x, k: pltpu.roll(x, n - k, 0)
    e0, e1 = ext[:, 0:LANES], ext[:, LANES:2 * LANES]
    lo = _lane_lo(e0.shape, POOL_DIM)
    a2 = e0 + back(e0, 1)
    a4 = a2 + back(a2, 2)
    col0 = jnp.where(lo, a2, ahead(a4, 1))
    b2 = e1 + back(e1, 1)
    b4 = b2 + back(b2, 2)
    b8 = b4 + back(b4, 4)
    b16 = b8 + back(b8, 8)
    col1 = jnp.where(lo, ahead(b8, 3), ahead(b16, 7))
    return jnp.concatenate([col0[HALO:HALO + TM], col1[HALO:HALO + TM]], axis=1)


def _merge_kernel(xl_ref, xc_ref, mod_ref, oal_ref, oac_ref, obl_ref, obc_ref, ocl_ref, occ_ref,
                  u_ref, up_ref, un_ref, gpre_ref, wgl_ref, wpool_ref, dscale_ref,
                  wbr_ref, wout_ref, gpost_ref, out_ref, *, lat_blocks, per_b):
    d = D_MODEL
    i = pl.program_id(0)
    is_ctx = i >= lat_blocks
    blk = i % per_b
    lane = lax.broadcasted_iota(jnp.int32, (TM, 2 * LANES), 1)
    grp = lane // POOL_DIM
    row = lax.broadcasted_iota(jnp.int32, (TM, 2 * LANES), 0)
    p_lo = jnp.left_shift(1, grp)
    n_tok = jnp.where(is_ctx, CTX_LEN, per_b * RB)
    zero_halo = jnp.zeros((HALO, 2 * LANES), F32)

    for sub in range(N_SUB):
        rows = slice(TM * sub, TM * (sub + 1))
        x = jnp.where(is_ctx, _sub_rows(xc_ref, sub), _sub_rows(xl_ref, sub))
        hb = (_rms(x, gpre_ref[...]) * (1.0 + mod_ref[:, d:2 * d]) + mod_ref[:, 0:d]).astype(BF16)

        u = u_ref[rows, :]
        if sub == 0:
            prev = jnp.where(jnp.logical_or(is_ctx, blk == 0), zero_halo, up_ref[...])
        else:
            prev = jnp.where(is_ctx, zero_halo, u_ref[TM * sub - HALO:TM * sub, :])
        if sub == N_SUB - 1:
            nxt = jnp.where(jnp.logical_or(is_ctx, blk == per_b - 1), zero_halo, un_ref[...])
        else:
            nxt = jnp.where(is_ctx, zero_halo, u_ref[TM * (sub + 1):TM * (sub + 1) + HALO, :])
        sums = _window_sums(jnp.concatenate([prev, u, nxt], axis=0))
        pos = jnp.where(is_ctx, row, blk * RB + TM * sub + row)
        cnt = jnp.minimum(pos + p_lo, n_tok) - jnp.maximum(pos - p_lo, 0)
        diff = sums / cnt.astype(F32) - u
        o_d = _dot(diff.astype(BF16), wpool_ref[...]) * dscale_ref[...]

        branches = (jnp.where(is_ctx, oac_ref[rows, :], oal_ref[rows, :]),
                    jnp.where(is_ctx, obc_ref[rows, :], obl_ref[rows, :]),
                    jnp.where(is_ctx, occ_ref[rows, :], ocl_ref[rows, :]),
                    o_d.astype(BF16))
        acc = jnp.zeros((TM, d), F32)
        for k in range(N_BRANCH):
            gl = _dot_t(hb, wgl_ref[d * k:d * (k + 1), :])
            proj = _dot(branches[k], wbr_ref[BRANCH_WIDTH * k:BRANCH_WIDTH * (k + 1), :])
            acc = acc + proj / (1.0 + jnp.exp(-gl))
        y = _dot(acc.astype(BF16), wout_ref[...])
        out_ref[rows, :] = x + mod_ref[:, 2 * d:3 * d] * _rms(y, gpost_ref[...])


def _merge_call(geo, l, n_blocks, x_lat, x_ctx, mods, o_pairs, u, wts):
    r = RB // HALO
    n_halo = geo.n_all // HALO
    up_spec = pl.BlockSpec((HALO, 256), lambda i: (jnp.maximum(i * r - 1, 0), 0))
    un_spec = pl.BlockSpec((HALO, 256), lambda i: (jnp.minimum((i + 1) * r, n_halo - 1), 0))
    o_specs, o_args = [], []
    for o_l, o_c in o_pairs:
        o_specs += [geo.lat_src(o_l), geo.ctx_src(o_c)]
        o_args += [o_l, o_c]
    consts = [wts[k] for k in ("mix_pre_g", "w_gl", "wpool", "d_scale", "w_branch", "w_out", "mix_post_g")]
    return pl.pallas_call(
        functools.partial(_merge_kernel, lat_blocks=geo.lat_blocks, per_b=geo.per_b),
        grid=(n_blocks,),
        in_specs=[geo.lat_src(x_lat), geo.ctx_src(x_ctx), geo.mod(l)] + o_specs
                 + [geo.rows(256), up_spec, un_spec] + [_resident(a, l) for a in consts],
        out_specs=geo.rows(D_MODEL),
        out_shape=jax.ShapeDtypeStruct((n_blocks * RB, D_MODEL), F32),
        compiler_params=_params(),
        name="merge",
    )(x_lat, x_ctx, mods, *o_args, u, u, u, *consts)


def _ffn_kernel(x_ref, mod_ref, gpre_ref, w1_ref, w3_ref, w2_ref, gpost_ref, out_ref):
    d = D_MODEL
    for sub in range(N_SUB):
        rows = slice(TM * sub, TM * (sub + 1))
        x = x_ref[rows, :]
        hb = (_rms(x, gpre_ref[...]) * (1.0 + mod_ref[:, 4 * d:5 * d]) + mod_ref[:, 3 * d:4 * d]).astype(BF16)
        a = _dot(hb, w1_ref[...])
        b = _dot(hb, w3_ref[...])
        act = (a / (1.0 + jnp.exp(-a)) * b).astype(BF16)
        f = _dot(act, w2_ref[...])
        out_ref[rows, :] = x + mod_ref[:, 5 * d:6 * d] * _rms(f, gpost_ref[...])


def _ffn_call(geo, l, x1, mods, wts):
    n, d = x1.shape
    consts = [wts[k] for k in ("ffn_pre_g", "w_ffn1", "w_ffn3", "w_ffn2", "ffn_post_g")]
    return pl.pallas_call(
        _ffn_kernel,
        grid=(n // RB,),
        in_specs=[geo.rows(d), geo.mod(l)] + [_resident(a, l) for a in consts],
        out_specs=geo.rows(d),
        out_shape=jax.ShapeDtypeStruct((n, d), F32),
        compiler_params=_params(),
        name="ffn",
    )(x1, mods, *consts)


def _rope_tables(n_tok):
    rows = n_tok // GRID_W
    row = jnp.repeat(jnp.arange(rows), GRID_W).astype(F32)
    col = jnp.tile(jnp.arange(GRID_W), rows).astype(F32)

    def cos_sin(rot_dim):
        n_freq = rot_dim // 4
        inv = ROPE_THETA ** (-jnp.arange(n_freq, dtype=F32) / n_freq)
        ang = jnp.concatenate([row[:, None] * inv, col[:, None] * inv], axis=-1)
        return jnp.cos(ang), jnp.sin(ang)

    c64, s64 = cos_sin(HEAD_DIM)
    cos_a = jnp.tile(c64, (1, 4))
    sin_a = jnp.concatenate([-s64, -s64, s64, s64], axis=-1)
    c32, s32 = cos_sin(B_ROPE)
    ones = lambda n: jnp.ones((n_tok, n), F32)
    zeros = lambda n: jnp.zeros((n_tok, n), F32)
    na, ra = B_NOPE // 2, B_ROPE // 2
    pad = LANES // 2 - na - ra
    cos_b = jnp.concatenate([ones(na), c32, ones(pad)] * 2, axis=-1)
    sin_b = jnp.concatenate([zeros(na), -s32, zeros(pad), zeros(na), s32, zeros(pad)], axis=-1)
    ident_c = jnp.ones((RB, LANES), F32)
    ident_s = jnp.zeros((RB, LANES), F32)
    return (jnp.concatenate([cos_a, ident_c]), jnp.concatenate([sin_a, ident_s]),
            jnp.concatenate([cos_b, ident_c]), jnp.concatenate([sin_b, ident_s]))


def _nope_rope_cols(nope, rope, lead):
    na, ra = B_NOPE // 2, B_ROPE // 2
    zeros = lambda n: jnp.zeros(lead + (n,), F32)
    out = []
    for s in range(2):
        out += [zeros(na) if nope is None else nope[..., na * s:na * (s + 1)],
                zeros(ra) if rope is None else rope[..., ra * s:ra * (s + 1)],
                zeros(LANES // 2 - na - ra)]
    return out


def _w_in_layouts(w_in):
    w_t = jnp.swapaxes(w_in, 1, 2)
    depth, _, d = w_t.shape
    half, na, ra = HEAD_DIM // 2, B_NOPE // 2, B_ROPE // 2
    zeros = lambda n: jnp.zeros((depth, n, d), F32)

    def split_half(c):
        return [w_t[:, c:c + half], w_t[:, c + HEAD_DIM:c + HEAD_DIM + half],
                w_t[:, c + half:c + HEAD_DIM], w_t[:, c + HEAD_DIM + half:c + 2 * HEAD_DIM]]

    kbr = w_t[:, 896:928]
    pad = LANES // 2 - na - ra
    rope_key = [zeros(na), kbr[:, :ra], zeros(pad), zeros(na), kbr[:, ra:], zeros(pad)]
    small = (split_half(0) + split_half(128) + split_half(256) + [w_t[:, 384:896]] + rope_key
             + split_half(928) + split_half(1056) + split_half(1184) + [w_t[:, 1312:1696]])
    return jnp.concatenate(small, axis=1).astype(BF16), w_t[:, 1696:].astype(BF16)


def _stacked_weights(mix_pre_g, mix_post_g, ffn_pre_g, ffn_post_g, w_in, a_qn_g, a_kn_g, b_qa_g, b_kva_g,
                     b_w_qb, b_w_kvb, d_w_pool, d_scale, w_branch, w_out, w_ffn1, w_ffn3, w_ffn2):
    depth, d, _ = w_in.shape
    zeros = lambda rows, n: jnp.zeros((depth, rows, n), F32)
    half = HEAD_DIM // 2
    nope_rope = lambda nope, rope, rows: _nope_rope_cols(nope, rope, (depth, rows))
    w_small, w_gl = _w_in_layouts(w_in)
    per_q = B_NOPE + B_ROPE
    per_kv = B_NOPE + B_V
    qcols, kvcols = [], []
    for hd in range(B_HEADS):
        qcols += nope_rope(b_w_qb[:, :, per_q * hd:per_q * hd + B_NOPE],
                           b_w_qb[:, :, per_q * hd + B_NOPE:per_q * (hd + 1)], B_Q_RANK)
        kvcols += nope_rope(b_w_kvb[:, :, per_kv * hd:per_kv * hd + B_NOPE], None, B_KV_RANK)
        kvcols += [b_w_kvb[:, :, per_kv * hd + B_NOPE:per_kv * (hd + 1)], zeros(B_KV_RANK, LANES - B_V)]
    pool_rows = [jnp.concatenate([zeros(POOL_DIM, POOL_DIM * g), d_w_pool[:, g],
                                  zeros(POOL_DIM, POOL_DIM * (POOL_GROUPS - 1 - g))], axis=2)
                 for g in range(POOL_GROUPS)]
    gain = lambda g: g[:, None, :]
    head_gain = lambda g: jnp.concatenate([g[:, :half], g[:, :half], g[:, half:], g[:, half:]], axis=1)
    return {
        "w_small": w_small,
        "w_gl": w_gl,
        "wqb": jnp.concatenate(qcols, axis=2).astype(BF16),
        "wkvb": jnp.concatenate(kvcols, axis=2).astype(BF16),
        "wpool": jnp.concatenate(pool_rows, axis=1).astype(BF16),
        "w_branch": w_branch.reshape(depth, N_BRANCH * BRANCH_WIDTH, d).astype(BF16),
        "w_out": w_out.astype(BF16),
        "w_ffn1": w_ffn1.astype(BF16), "w_ffn3": w_ffn3.astype(BF16), "w_ffn2": w_ffn2.astype(BF16),
        "mix_pre_g": gain(mix_pre_g), "mix_post_g": gain(mix_post_g),
        "ffn_pre_g": gain(ffn_pre_g), "ffn_post_g": gain(ffn_post_g),
        "a_qn_g": gain(head_gain(a_qn_g)), "a_kn_g": gain(head_gain(a_kn_g)),
        "b_qa_g": gain(b_qa_g), "b_kva_g": gain(b_kva_g), "d_scale": gain(d_scale),
    }


def kernel(x, c, ctx, c_ctx, ada_w, ada_b, mix_pre_g, mix_post_g, ffn_pre_g, ffn_post_g, w_in, a_qn_g, a_kn_g,
           b_qa_g, b_kva_g, b_w_qb, b_w_kvb, c_sink, d_w_pool, d_scale, w_branch, w_out, w_ffn1, w_ffn3, w_ffn2):
    bsz, n_tok, d = x.shape
    depth = ada_w.shape[0]
    assert d == D_MODEL and ctx.shape[1] == CTX_LEN and bsz + 1 <= MOD_ROWS
    geo = _Rows(bsz, n_tok)

    cs = jnp.zeros((MOD_ROWS, d), F32).at[:bsz].set(c).at[bsz].set(c_ctx)
    mods = _mods_call(cs, ada_w, ada_b)[:, :, None, :]
    tabs = _rope_tables(n_tok)
    wts = _stacked_weights(mix_pre_g, mix_post_g, ffn_pre_g, ffn_post_g, w_in, a_qn_g, a_kn_g, b_qa_g, b_kva_g,
                           b_w_qb, b_w_kvb, d_w_pool, d_scale, w_branch, w_out, w_ffn1, w_ffn3, w_ffn2)

    x_lat, x_ctx = x, ctx
    for l in range(depth):
        with_ctx = l < depth - 1
        n_blocks = geo.all_blocks if with_ctx else geo.lat_blocks
        qa, kva, qb, kb, vb, qc, kvc, u = _pre_call(geo, l, x_lat, x_ctx, mods, wts, tabs)
        oa = _dense_attn_call("attn_a", geo, qa, [kva], _HEADS_A, with_ctx)
        ob = _dense_attn_call("attn_b", geo, qb, [kb, vb], _HEADS_B, with_ctx)
        oc = _window_attn_call(geo, l, c_sink, qc, kvc, with_ctx)
        x1 = _merge_call(geo, l, n_blocks, x_lat, x_ctx, mods, [oa, ob, oc], u, wts)
        x_lat = x_ctx = _ffn_call(geo, l, x1, mods, wts)
    return x_lat.reshape(bsz, n_tok, d)
```

```python
import functools

import jax
import jax.numpy as jnp
from jax import lax
from jax.experimental import pallas as pl
from jax.experimental.pallas import tpu as pltpu

D_MODEL = 1024
CTX_LEN = 256
GRID_W = 64
HEAD_DIM = 64
ROPE_THETA = 10000.0
EPS = 1e-6
N_BRANCH = 4
BRANCH_WIDTH = 256
B_HEADS = 4
B_Q_RANK = 256
B_KV_RANK = 128
B_NOPE = 64
B_ROPE = 32
B_V = 64
WINDOW = 128
POOL_WINDOWS = (2, 4, 8, 16)
POOL_GROUPS = 4
POOL_DIM = 64
LOG2E = 1.4426950408889634

LANES = 128
TM = CTX_LEN
N_SUB = 4
RB = N_SUB * TM
HALO = 8
BAND = 2 * TM
TQ = TM
N_ATTN_BUF = 4
STAGE_GAP = 1
N_SMALL = 1792
MOD_ROWS = 16
VMEM_LIMIT = 56 * 1024 * 1024

F32 = jnp.float32
BF16 = jnp.bfloat16


def _dot(a, b):
    return jnp.dot(a, b, preferred_element_type=F32)


def _dot_t(a, b):
    return lax.dot_general(a, b, (((1,), (1,)), ((), ())), preferred_element_type=F32)


def _rms(x, g):
    return x * lax.rsqrt(jnp.mean(x * x, axis=-1, keepdims=True) + EPS) * g


def _lane_lo(shape, width):
    lane = lax.broadcasted_iota(jnp.int32, shape, 1)
    return (lane % (2 * width)) < width


def _head0(shape):
    return _lane_lo(shape, HEAD_DIM // 2)


def _rope(x, cos, sin):
    return x * cos + pltpu.roll(x, LANES // 2, 1) * sin


def _head_rms(x, g):
    h0 = _head0(x.shape)
    s = x * x
    s0 = jnp.sum(jnp.where(h0, s, 0.0), axis=-1, keepdims=True)
    s1 = jnp.sum(jnp.where(h0, 0.0, s), axis=-1, keepdims=True)
    ms = jnp.where(h0, s0, s1) * (1.0 / HEAD_DIM)
    return x * lax.rsqrt(ms + EPS) * g


def _pair_layouts(k, v):
    h0 = _head0(k.shape)
    lo = _lane_lo(v.shape, HEAD_DIM)
    k_up = pltpu.roll(k, HEAD_DIM // 2, 1)
    k_dn = pltpu.roll(k, LANES - HEAD_DIM // 2, 1)
    vr = pltpu.roll(v, HEAD_DIM, 1)
    return (jnp.where(h0, k, k_up), jnp.where(lo, v, 1.0), jnp.where(h0, k_dn, k), jnp.where(lo, vr, 1.0))


def _resident(arr, layer=None):
    if layer is None:
        nd = arr.ndim
        return pl.BlockSpec(arr.shape, lambda *_: (0,) * nd, pipeline_mode=pl.Buffered(1))
    nd = arr.ndim - 1
    return pl.BlockSpec((None,) + arr.shape[1:], lambda *_: (layer,) + (0,) * nd, pipeline_mode=pl.Buffered(1))


def _params(n_axes=1):
    return pltpu.CompilerParams(dimension_semantics=("arbitrary",) * n_axes, vmem_limit_bytes=VMEM_LIMIT)


class _Rows:
    def __init__(self, bsz, n_tok):
        assert n_tok % RB == 0 and (bsz * CTX_LEN) % RB == 0
        self.bsz, self.n_tok = bsz, n_tok
        self.n_lat = bsz * n_tok
        self.n_all = self.n_lat + bsz * CTX_LEN
        self.per_b = n_tok // RB
        self.lat_blocks = self.n_lat // RB
        self.all_blocks = self.n_all // RB

    def rows(self, width):
        return pl.BlockSpec((RB, width), lambda i: (i, 0))

    def lat_src(self, arr):
        last = self.lat_blocks - 1
        if arr.ndim == 3:
            return pl.BlockSpec((None, RB, arr.shape[2]),
                                lambda i: (jnp.minimum(i, last) // self.per_b, jnp.minimum(i, last) % self.per_b, 0))
        return pl.BlockSpec((RB, arr.shape[1]), lambda i: (jnp.minimum(i, last), 0))

    def ctx_src(self, arr):
        ctx_last = self.all_blocks - self.lat_blocks - 1
        rel = lambda i: jnp.clip(i - self.lat_blocks, 0, ctx_last)
        if arr.ndim == 3:
            return pl.BlockSpec((N_SUB, CTX_LEN, arr.shape[2]), lambda i: (rel(i), 0, 0))
        base = (arr.shape[0] - self.bsz * CTX_LEN) // RB
        return pl.BlockSpec((RB, arr.shape[1]), lambda i: (base + rel(i), 0))

    def mod(self, layer):
        return pl.BlockSpec((None, None, 1, 6 * D_MODEL),
                            lambda i: (layer, jnp.where(i < self.lat_blocks, i // self.per_b, self.bsz), 0, 0))

    def table(self):
        return pl.BlockSpec((RB, LANES), lambda i: (jnp.where(i < self.lat_blocks, i % self.per_b, self.per_b), 0))


def _mods_kernel(c_ref, w_ref, b_ref, o_ref):
    c = c_ref[...]
    s = c / (1.0 + jnp.exp(-c))
    o_ref[...] = _dot(s.astype(BF16), w_ref[...].astype(BF16)) + b_ref[...]


def _mods_call(cs, ada_w, ada_b):
    depth, d, n = ada_w.shape
    nb = 1536
    return pl.pallas_call(
        _mods_kernel,
        grid=(depth, n // nb),
        in_specs=[pl.BlockSpec((MOD_ROWS, d), lambda l, j: (0, 0)),
                  pl.BlockSpec((None, d, nb), lambda l, j: (l, 0, j)),
                  pl.BlockSpec((None, 1, nb), lambda l, j: (l, 0, j))],
        out_specs=pl.BlockSpec((None, MOD_ROWS, nb), lambda l, j: (l, 0, j)),
        out_shape=jax.ShapeDtypeStruct((depth, MOD_ROWS, n), F32),
        compiler_params=_params(2),
        name="adaln_mods",
    )(cs, ada_w, ada_b.reshape(depth, 1, n))


def _sub_rows(ref, sub):
    return ref[sub] if len(ref.shape) == 3 else ref[TM * sub:TM * (sub + 1), :]


def _pre_kernel(xl_ref, xc_ref, mod_ref, g_ref, w_ref, gq_ref, gk_ref, gqa_ref, gkva_ref, wqb_ref, wkvb_ref,
                cosa_ref, sina_ref, cosb_ref, sinb_ref,
                qa_ref, kva_ref, qb_ref, kb_ref, vb_ref, qc_ref, kvc_ref, u_ref, *, lat_blocks):
    d = D_MODEL
    is_ctx = pl.program_id(0) >= lat_blocks
    sc_a = HEAD_DIM ** -0.5 * LOG2E
    sc_b = (B_NOPE + B_ROPE) ** -0.5 * LOG2E
    for sub in range(N_SUB):
        rows = slice(TM * sub, TM * (sub + 1))
        x = jnp.where(is_ctx, _sub_rows(xc_ref, sub), _sub_rows(xl_ref, sub))
        h = _rms(x, g_ref[...]) * (1.0 + mod_ref[:, d:2 * d]) + mod_ref[:, 0:d]
        z = _dot_t(h.astype(BF16), w_ref[...])
        cosa, sina = cosa_ref[rows, :], sina_ref[rows, :]
        cosb, sinb = cosb_ref[rows, :], sinb_ref[rows, :]

        for j in range(2):
            q = _head_rms(z[:, LANES * j:LANES * (j + 1)], gq_ref[...])
            qa_ref[rows, LANES * j:LANES * (j + 1)] = (_rope(q, cosa, sina) * sc_a).astype(BF16)
        k = _rope(_head_rms(z[:, 256:384], gk_ref[...]), cosa, sina)
        for i, blk in enumerate(_pair_layouts(k, z[:, 384:512])):
            kva_ref[rows, LANES * i:LANES * (i + 1)] = blk.astype(BF16)

        qn = _rms(z[:, 512:768], gqa_ref[...]).astype(BF16)
        qb = _dot(qn, wqb_ref[...]) * sc_b
        kvn = _rms(z[:, 768:896], gkva_ref[...]).astype(BF16)
        kvb = _dot(kvn, wkvb_ref[...])
        kbr = _rope(z[:, 896:1024], cosb, sinb)
        lo = _lane_lo(kbr.shape, HEAD_DIM)
        for hd in range(B_HEADS):
            c0 = LANES * hd
            qb_ref[rows, c0:c0 + LANES] = _rope(qb[:, c0:c0 + LANES], cosb, sinb).astype(BF16)
            kb_ref[rows, c0:c0 + LANES] = (kvb[:, 2 * c0:2 * c0 + LANES] + kbr).astype(BF16)
            vb_ref[rows, c0:c0 + LANES] = jnp.where(lo, kvb[:, 2 * c0 + LANES:2 * c0 + 2 * LANES], 1.0).astype(BF16)

        for j in range(2):
            q = z[:, 1024 + LANES * j:1024 + LANES * (j + 1)]
            qc_ref[rows, LANES * j:LANES * (j + 1)] = (_rope(q, cosa, sina) * sc_a).astype(BF16)
        k = _rope(z[:, 1280:1408], cosa, sina)
        for i, blk in enumerate(_pair_layouts(k, z[:, 1408:1536])):
            kvc_ref[rows, LANES * i:LANES * (i + 1)] = blk.astype(BF16)

        u_ref[rows, :] = z[:, 1536:1792]


def _pre_call(geo, l, x_lat, x_ctx, mods, wts, tabs):
    widths = (256, 512, 512, 512, 512, 256, 512)
    out_shape = [jax.ShapeDtypeStruct((geo.n_all, w), BF16) for w in widths]
    out_shape.append(jax.ShapeDtypeStruct((geo.n_all, 256), F32))
    consts = [wts[k] for k in ("mix_pre_g", "w_small", "a_qn_g", "a_kn_g", "b_qa_g", "b_kva_g", "wqb", "wkvb")]
    return pl.pallas_call(
        functools.partial(_pre_kernel, lat_blocks=geo.lat_blocks),
        grid=(geo.all_blocks,),
        in_specs=[geo.lat_src(x_lat), geo.ctx_src(x_ctx), geo.mod(l)] + [_resident(a, l) for a in consts]
                 + [geo.table()] * 4,
        out_specs=[geo.rows(w) for w in widths] + [geo.rows(256)],
        out_shape=out_shape,
        compiler_params=_params(),
        name="pre_attn",
    )(x_lat, x_ctx, mods, *consts, *tabs)


def _probs(scores, extra=None):
    m = functools.reduce(jnp.maximum, [jnp.max(s, axis=-1, keepdims=True) for s in scores])
    if extra is not None:
        m = jnp.maximum(m, extra)
    ps = [jnp.exp2(s - m).astype(BF16) for s in scores]
    return ps, (None if extra is None else jnp.exp2(extra - m))


def _softmax_rows(s_ref, p_ref, extra=None):
    n = s_ref.shape[1]
    m = s_ref[:, 0:LANES]
    for c in range(1, n // LANES):
        m = jnp.maximum(m, s_ref[:, LANES * c:LANES * (c + 1)])
    m = jnp.max(m, axis=-1, keepdims=True)
    if extra is not None:
        m = jnp.maximum(m, extra)
    mb = jnp.broadcast_to(m, (s_ref.shape[0], LANES))
    for c in range(n // (2 * LANES)):
        cols = slice(2 * LANES * c, 2 * LANES * (c + 1))
        s = s_ref[:, cols]
        p_ref[:, cols] = jnp.concatenate([jnp.exp2(s[:, :LANES] - mb), jnp.exp2(s[:, LANES:] - mb)],
                                         axis=1).astype(BF16)
    return None if extra is None else jnp.exp2(extra - m)


def _normalise_pair(r0, r1, e0=None, e1=None):
    lo = _lane_lo(r0.shape, HEAD_DIM)
    r0r = pltpu.roll(r0, HEAD_DIM, 1)
    r1r = pltpu.roll(r1, HEAD_DIM, 1)
    den0 = r0r if e0 is None else r0r + e0
    den1 = r1 if e1 is None else r1 + e1
    return jnp.where(lo, r0 / den0, r1r / den1)


def _stack_pair(q):
    h0 = _head0(q.shape)
    zero = jnp.zeros_like(q)
    return jnp.concatenate([jnp.where(h0, q, zero), jnp.where(h0, zero, q)], axis=0)


def _dense_attn_kernel(*refs, heads, n_q, n_kv, with_ctx):
    q_lats, q_ctxs = refs[0:2 * n_q:2], refs[1:2 * n_q:2]
    kv_lat = refs[2 * n_q:2 * (n_q + n_kv):2]
    kv_ctx = refs[2 * n_q + 1:2 * (n_q + n_kv):2]
    outs = refs[2 * (n_q + n_kv):2 * (2 * n_q + n_kv)]
    o_lats, o_ctxs = outs[0::2], outs[1::2]
    scratch = refs[2 * (2 * n_q + n_kv):]
    n_heads = len(heads)
    s_bufs, p_bufs = scratch[0:N_ATTN_BUF], scratch[N_ATTN_BUF:2 * N_ATTN_BUF]
    r_bufs = scratch[2 * N_ATTN_BUF:]
    n_tok = q_lats[0].shape[0]
    n_tiles = n_tok // TQ

    def q_block(q_refs, row0, h, rows=TQ):
        qarr, qcol, mask, *_ = heads[h]
        q = q_refs[qarr][pl.ds(row0, rows), LANES * qcol:LANES * (qcol + 1)]
        if mask is None:
            return q
        h0 = _head0(q.shape)
        return jnp.where(h0 if mask == "head0" else ~h0, q, jnp.zeros_like(q))

    def keys(refs_, h):
        karr, kcol = heads[h][3:5]
        return refs_[karr][:, LANES * kcol:LANES * (kcol + 1)]

    def values(refs_, h):
        varr, vcol = heads[h][5:7]
        return refs_[varr][:, LANES * vcol:LANES * (vcol + 1)]

    def store_pair(o_refs, row0, pair, r0, r1):
        j = pair % 2
        o_refs[pair // 2][pl.ds(row0, r0.shape[0]), LANES * j:LANES * (j + 1)] = (
            _normalise_pair(r0, r1).astype(BF16))

    if with_ctx:
        rs = []
        for h in range(n_heads):
            (p,), _ = _probs([_dot_t(q_block(q_ctxs, 0, h, CTX_LEN), keys(kv_ctx, h))])
            rs.append(_dot(p, values(kv_ctx, h)))
        for pair in range(n_heads // 2):
            store_pair(o_ctxs, 0, pair, rs[2 * pair], rs[2 * pair + 1])

    def scores(row0, h, buf):
        q = q_block(q_lats, row0, h)
        s_bufs[buf][:, 0:CTX_LEN] = _dot_t(q, keys(kv_ctx, h))
        s_bufs[buf][:, CTX_LEN:] = _dot_t(q, keys(kv_lat, h))

    def softmax(buf):
        _softmax_rows(s_bufs[buf], p_bufs[buf])

    def weighted(h, buf):
        varr, vcol = heads[h][5:7]
        base = vcol // 2 * 2
        pair = slice(LANES * base, LANES * (base + 2))
        r2 = (_dot(p_bufs[buf][:, 0:CTX_LEN], kv_ctx[varr][:, pair])
              + _dot(p_bufs[buf][:, CTX_LEN:], kv_lat[varr][:, pair]))
        return r2[:, LANES * (vcol - base):LANES * (vcol - base + 1)]

    assert n_heads % N_ATTN_BUF == 0 and 2 * STAGE_GAP < N_ATTN_BUF
    buf_of = lambda h: h % N_ATTN_BUF
    for h in range(n_heads - STAGE_GAP, n_heads):
        p_bufs[buf_of(h)][...] = jnp.zeros(p_bufs[0].shape, BF16)
    for rb in r_bufs:
        rb[...] = jnp.zeros(rb.shape, F32)
    for h in range(STAGE_GAP):
        scores(0, h, buf_of(h))

    def finish(row0, hp):
        r = weighted(hp, buf_of(hp))
        pair = hp // 2
        if hp % 2 == 0:
            r_bufs[pair % len(r_bufs)][...] = r
        else:
            store_pair(o_lats, row0, pair, r_bufs[pair % len(r_bufs)][...], r)

    def tile_body(i, carry):
        row0 = pl.multiple_of(i * TQ, TQ)
        prev0 = pl.multiple_of(jnp.maximum(i - 1, 0) * TQ, TQ)
        next0 = pl.multiple_of(jnp.minimum(i + 1, n_tiles - 1) * TQ, TQ)
        for h in range(n_heads):
            ahead = h + STAGE_GAP
            scores(row0 if ahead < n_heads else next0, ahead % n_heads, buf_of(ahead))
            softmax(buf_of(h))
            behind = h - STAGE_GAP
            finish(row0 if behind >= 0 else prev0, behind % n_heads)
        return carry

    lax.fori_loop(0, n_tiles, tile_body, 0)
    for h in range(n_heads - STAGE_GAP, n_heads):
        finish(n_tok - TQ, h)
    if not with_ctx:
        for o_ctx in o_ctxs:
            o_ctx[...] = jnp.zeros(o_ctx.shape, BF16)


_HEADS_AB = (tuple((0, h // 2, "head0" if h % 2 == 0 else "head1", 0, 2 * (h // 2), 0, 2 * (h // 2) + 1)
                   for h in range(4))
             + tuple((1, h, None, 1, h, 2, h) for h in range(4)))


def _batch_specs(geo, width):
    ctx0 = geo.n_lat // CTX_LEN
    return [pl.BlockSpec((geo.n_tok, width), lambda b: (b, 0)),
            pl.BlockSpec((CTX_LEN, width), lambda b: (ctx0 + b, 0))]


def _attn_out(geo):
    specs = [pl.BlockSpec((geo.n_tok, BRANCH_WIDTH), lambda b: (b, 0)),
             pl.BlockSpec((CTX_LEN, BRANCH_WIDTH), lambda b: (b, 0))]
    shapes = [jax.ShapeDtypeStruct((geo.n_lat, BRANCH_WIDTH), BF16),
              jax.ShapeDtypeStruct((geo.bsz * CTX_LEN, BRANCH_WIDTH), BF16)]
    return specs, shapes


def _dense_attn_call(name, geo, qs, kvs, heads, with_ctx):
    s_len = geo.n_tok + CTX_LEN
    in_specs, args = [], []
    for arr in list(qs) + list(kvs):
        in_specs += _batch_specs(geo, arr.shape[1])
        args += [arr, arr]
    out_specs, out_shape = [], []
    for _ in qs:
        specs, shapes = _attn_out(geo)
        out_specs += specs
        out_shape += shapes
    outs = pl.pallas_call(
        functools.partial(_dense_attn_kernel, heads=heads, n_q=len(qs), n_kv=len(kvs), with_ctx=with_ctx),
        grid=(geo.bsz,),
        in_specs=in_specs,
        out_specs=out_specs,
        out_shape=out_shape,
        scratch_shapes=[pltpu.VMEM((TQ, s_len), F32)] * N_ATTN_BUF + [pltpu.VMEM((TQ, s_len), BF16)] * N_ATTN_BUF
                       + [pltpu.VMEM((TQ, LANES), F32)] * 2,
        compiler_params=_params(),
        name=name,
    )(*args)
    return [outs[2 * m:2 * m + 2] for m in range(len(qs))]


def _window_attn_kernel(sink_ref, q_lat, q_ctx, kv_lat, kv_ctx, o_lat, o_ctx, *scratch, with_ctx, layer):
    s_bufs, p_bufs, e_bufs, bias_ref = scratch[0:4], scratch[4:8], scratch[8:12], scratch[12]
    n_tok = q_lat.shape[0]
    n_tiles = n_tok // TM
    assert n_tiles >= 2

    row = lax.broadcasted_iota(jnp.int32, (2 * TM, BAND), 0)
    col = lax.broadcasted_iota(jnp.int32, (2 * TM, BAND), 1)
    rel0 = col - jnp.where(row < TM, row, row - TM)
    for place, off in enumerate((0, -WINDOW, TM - BAND)):
        bias_ref[place] = jnp.where(jnp.abs(rel0 + off) <= WINDOW, 0.0, -1e30)

    def sink_col(j):
        row = lax.broadcasted_iota(jnp.int32, (2 * TM, 1), 0)
        return jnp.where(row < TM, sink_ref[layer, 2 * j], sink_ref[layer, 2 * j + 1]) * LOG2E

    def cols(j):
        return slice(2 * LANES * j, 2 * LANES * j + LANES), slice(2 * LANES * j + LANES, 2 * LANES * (j + 1))

    def store_pair(ref, row0, j, r, e):
        ref[pl.ds(row0, TM), LANES * j:LANES * (j + 1)] = _normalise_pair(r[:TM], r[TM:], e[:TM], e[TM:]).astype(BF16)

    if with_ctx:
        for j in range(2):
            c_kk, c_v1 = cols(j)
            (p,), e = _probs([_dot_t(_stack_pair(q_ctx[:, LANES * j:LANES * (j + 1)]), kv_ctx[:, c_kk])], sink_col(j))
            store_pair(o_ctx, 0, j, _dot(p, kv_ctx[:, c_v1]), e)

    def band_start(i):
        if isinstance(i, int):
            return min(max(i * TM - WINDOW, 0), n_tok - BAND)
        return pl.multiple_of(jnp.clip(i * TM - WINDOW, 0, n_tok - BAND), WINDOW)

    def scores(i, j, buf):
        c_kk, _ = cols(j)
        row0 = i * TM if isinstance(i, int) else pl.multiple_of(i * TM, TM)
        qs = _stack_pair(q_lat[pl.ds(row0, TM), LANES * j:LANES * (j + 1)])
        s_bufs[buf][:, 0:CTX_LEN] = _dot_t(qs, kv_ctx[:, c_kk])
        if isinstance(i, int):
            place = 0 if i == 0 else 2 if i == n_tiles - 1 else 1
        else:
            place = jnp.where(i == 0, 0, jnp.where(i == n_tiles - 1, 2, 1))
        s_bufs[buf][:, CTX_LEN:] = _dot_t(qs, kv_lat[pl.ds(band_start(i), BAND), c_kk]) + bias_ref[place]

    def softmax(i, j, buf):
        e = _softmax_rows(s_bufs[buf], p_bufs[buf], sink_col(j))
        e_bufs[buf][...] = jnp.broadcast_to(e, e_bufs[buf].shape)

    def weighted(i, j, buf):
        both = slice(2 * LANES * j, 2 * LANES * (j + 1))
        r2 = (_dot(p_bufs[buf][:, 0:CTX_LEN], kv_ctx[:, both])
              + _dot(p_bufs[buf][:, CTX_LEN:], kv_lat[pl.ds(band_start(i), BAND), both]))
        return r2[:, LANES:]

    def finish(i, j, buf):
        row0 = i * TM if isinstance(i, int) else pl.multiple_of(i * TM, TM)
        store_pair(o_lat, row0, j, weighted(i, j, buf), e_bufs[buf][...])

    def pair_of_tiles(i0, first, last):
        i1 = i0 + 1
        for c, (i, j) in enumerate(((i0, 0), (i0, 1), (i1, 0), (i1, 1))):
            if c < 2:
                scores(i1, j, c + 2)
            elif not last:
                scores(i0 + 2, j, c - 2)
            softmax(i, j, c)
            if c >= 2:
                finish(i0, j, c - 2)
            elif not first:
                finish(i0 - 1, j, c + 2)

    assert n_tiles % 2 == 0 and n_tiles >= 4
    scores(0, 0, 0)
    scores(0, 1, 1)
    pair_of_tiles(0, True, False)

    def body(k, carry):
        pair_of_tiles(2 * k, False, False)
        return carry

    lax.fori_loop(1, n_tiles // 2 - 1, body, 0)
    pair_of_tiles(n_tiles - 2, False, True)
    finish(n_tiles - 1, 0, 2)
    finish(n_tiles - 1, 1, 3)
    if not with_ctx:
        o_ctx[...] = jnp.zeros(o_ctx.shape, BF16)


def _window_attn_call(geo, l, sink, q, kv, with_ctx):
    n_keys = CTX_LEN + BAND
    out_specs, out_shape = _attn_out(geo)
    return pl.pallas_call(
        functools.partial(_window_attn_kernel, with_ctx=with_ctx, layer=l),
        grid=(geo.bsz,),
        in_specs=[pl.BlockSpec(memory_space=pltpu.SMEM)] + _batch_specs(geo, q.shape[1]) + _batch_specs(geo, kv.shape[1]),
        out_specs=out_specs,
        out_shape=out_shape,
        scratch_shapes=[pltpu.VMEM((2 * TM, n_keys), F32)] * 4 + [pltpu.VMEM((2 * TM, n_keys), BF16)] * 4
                       + [pltpu.VMEM((2 * TM, LANES), F32)] * 4 + [pltpu.VMEM((3, 2 * TM, BAND), F32)],
        compiler_params=_params(),
        name="attn_c",
    )(sink, q, q, kv, kv)


def _window_sums(ext):
    assert POOL_WINDOWS == (2, 4, 8, 16) and HALO >= 8
    n = ext.shape[0]
    back = lambda x, k: pltpu.roll(x, k, 0)
    ahead = lambda x, k: pltpu.roll(x, n - k, 0)
    e0, e1 = ext[:, 0:LANES], ext[:, LANES:2 * LANES]
    lo = _lane_lo(e0.shape, POOL_DIM)
    a2 = e0 + back(e0, 1)
    a4 = a2 + back(a2, 2)
    col0 = jnp.where(lo, a2, ahead(a4, 1))
    b2 = e1 + back(e1, 1)
    b4 = b2 + back(b2, 2)
    b8 = b4 + back(b4, 4)
    b16 = b8 + back(b8, 8)
    col1 = jnp.where(lo, ahead(b8, 3), ahead(b16, 7))
    return jnp.concatenate([col0[HALO:HALO + TM], col1[HALO:HALO + TM]], axis=1)


def _merge_kernel(xl_ref, xc_ref, mod_ref, oal_ref, oac_ref, obl_ref, obc_ref, ocl_ref, occ_ref,
                  u_ref, up_ref, un_ref, gpre_ref, wgl_ref, wpool_ref, dscale_ref,
                  wbr_ref, wout_ref, gpost_ref, out_ref, *, lat_blocks, per_b):
    d = D_MODEL
    i = pl.program_id(0)
    is_ctx = i >= lat_blocks
    blk = i % per_b
    lane = lax.broadcasted_iota(jnp.int32, (TM, 2 * LANES), 1)
    grp = lane // POOL_DIM
    row = lax.broadcasted_iota(jnp.int32, (TM, 2 * LANES), 0)
    p_lo = jnp.left_shift(1, grp)
    n_tok = jnp.where(is_ctx, CTX_LEN, per_b * RB)
    zero_halo = jnp.zeros((HALO, 2 * LANES), F32)

    for sub in range(N_SUB):
        rows = slice(TM * sub, TM * (sub + 1))
        x = jnp.where(is_ctx, _sub_rows(xc_ref, sub), _sub_rows(xl_ref, sub))
        hb = (_rms(x, gpre_ref[...]) * (1.0 + mod_ref[:, d:2 * d]) + mod_ref[:, 0:d]).astype(BF16)

        u = u_ref[rows, :]
        if sub == 0:
            prev = jnp.where(jnp.logical_or(is_ctx, blk == 0), zero_halo, up_ref[...])
        else:
            prev = jnp.where(is_ctx, zero_halo, u_ref[TM * sub - HALO:TM * sub, :])
        if sub == N_SUB - 1:
            nxt = jnp.where(jnp.logical_or(is_ctx, blk == per_b - 1), zero_halo, un_ref[...])
        else:
            nxt = jnp.where(is_ctx, zero_halo, u_ref[TM * (sub + 1):TM * (sub + 1) + HALO, :])
        sums = _window_sums(jnp.concatenate([prev, u, nxt], axis=0))
        pos = jnp.where(is_ctx, row, blk * RB + TM * sub + row)
        cnt = jnp.minimum(pos + p_lo, n_tok) - jnp.maximum(pos - p_lo, 0)
        diff = sums / cnt.astype(F32) - u
        o_d = _dot(diff.astype(BF16), wpool_ref[...]) * dscale_ref[...]

        branches = (jnp.where(is_ctx, oac_ref[rows, :], oal_ref[rows, :]),
                    jnp.where(is_ctx, obc_ref[rows, :], obl_ref[rows, :]),
                    jnp.where(is_ctx, occ_ref[rows, :], ocl_ref[rows, :]),
                    o_d.astype(BF16))
        acc = jnp.zeros((TM, d), F32)
        for k in range(N_BRANCH):
            gl = _dot_t(hb, wgl_ref[d * k:d * (k + 1), :])
            proj = _dot(branches[k], wbr_ref[BRANCH_WIDTH * k:BRANCH_WIDTH * (k + 1), :])
            acc = acc + proj / (1.0 + jnp.exp(-gl))
        y = _dot(acc.astype(BF16), wout_ref[...])
        out_ref[rows, :] = x + mod_ref[:, 2 * d:3 * d] * _rms(y, gpost_ref[...])


def _merge_call(geo, l, n_blocks, x_lat, x_ctx, mods, o_pairs, u, wts):
    r = RB // HALO
    n_halo = geo.n_all // HALO
    up_spec = pl.BlockSpec((HALO, 256), lambda i: (jnp.maximum(i * r - 1, 0), 0))
    un_spec = pl.BlockSpec((HALO, 256), lambda i: (jnp.minimum((i + 1) * r, n_halo - 1), 0))
    o_specs, o_args = [], []
    for o_l, o_c in o_pairs:
        o_specs += [geo.lat_src(o_l), geo.ctx_src(o_c)]
        o_args += [o_l, o_c]
    consts = [wts[k] for k in ("mix_pre_g", "w_gl", "wpool", "d_scale", "w_branch", "w_out", "mix_post_g")]
    return pl.pallas_call(
        functools.partial(_merge_kernel, lat_blocks=geo.lat_blocks, per_b=geo.per_b),
        grid=(n_blocks,),
        in_specs=[geo.lat_src(x_lat), geo.ctx_src(x_ctx), geo.mod(l)] + o_specs
                 + [geo.rows(256), up_spec, un_spec] + [_resident(a, l) for a in consts],
        out_specs=geo.rows(D_MODEL),
        out_shape=jax.ShapeDtypeStruct((n_blocks * RB, D_MODEL), F32),
        compiler_params=_params(),
        name="merge",
    )(x_lat, x_ctx, mods, *o_args, u, u, u, *consts)


def _ffn_kernel(x_ref, mod_ref, gpre_ref, w1_ref, w3_ref, w2_ref, gpost_ref, out_ref):
    d = D_MODEL
    for sub in range(N_SUB):
        rows = slice(TM * sub, TM * (sub + 1))
        x = x_ref[rows, :]
        hb = (_rms(x, gpre_ref[...]) * (1.0 + mod_ref[:, 4 * d:5 * d]) + mod_ref[:, 3 * d:4 * d]).astype(BF16)
        a = _dot(hb, w1_ref[...])
        b = _dot(hb, w3_ref[...])
        act = (a / (1.0 + jnp.exp(-a)) * b).astype(BF16)
        f = _dot(act, w2_ref[...])
        out_ref[rows, :] = x + mod_ref[:, 5 * d:6 * d] * _rms(f, gpost_ref[...])


def _ffn_call(geo, l, x1, mods, wts):
    n, d = x1.shape
    consts = [wts[k] for k in ("ffn_pre_g", "w_ffn1", "w_ffn3", "w_ffn2", "ffn_post_g")]
    return pl.pallas_call(
        _ffn_kernel,
        grid=(n // RB,),
        in_specs=[geo.rows(d), geo.mod(l)] + [_resident(a, l) for a in consts],
        out_specs=geo.rows(d),
        out_shape=jax.ShapeDtypeStruct((n, d), F32),
        compiler_params=_params(),
        name="ffn",
    )(x1, mods, *consts)


def _rope_tables(n_tok):
    rows = n_tok // GRID_W
    row = jnp.repeat(jnp.arange(rows), GRID_W).astype(F32)
    col = jnp.tile(jnp.arange(GRID_W), rows).astype(F32)

    def cos_sin(rot_dim):
        n_freq = rot_dim // 4
        inv = ROPE_THETA ** (-jnp.arange(n_freq, dtype=F32) / n_freq)
        ang = jnp.concatenate([row[:, None] * inv, col[:, None] * inv], axis=-1)
        return jnp.cos(ang), jnp.sin(ang)

    c64, s64 = cos_sin(HEAD_DIM)
    cos_a = jnp.tile(c64, (1, 4))
    sin_a = jnp.concatenate([-s64, -s64, s64, s64], axis=-1)
    c32, s32 = cos_sin(B_ROPE)
    ones = lambda n: jnp.ones((n_tok, n), F32)
    zeros = lambda n: jnp.zeros((n_tok, n), F32)
    na, ra = B_NOPE // 2, B_ROPE // 2
    pad = LANES // 2 - na - ra
    cos_b = jnp.concatenate([ones(na), c32, ones(pad)] * 2, axis=-1)
    sin_b = jnp.concatenate([zeros(na), -s32, zeros(pad), zeros(na), s32, zeros(pad)], axis=-1)
    ident_c = jnp.ones((RB, LANES), F32)
    ident_s = jnp.zeros((RB, LANES), F32)
    return (jnp.concatenate([cos_a, ident_c]), jnp.concatenate([sin_a, ident_s]),
            jnp.concatenate([cos_b, ident_c]), jnp.concatenate([sin_b, ident_s]))


def _nope_rope_cols(nope, rope, lead):
    na, ra = B_NOPE // 2, B_ROPE // 2
    zeros = lambda n: jnp.zeros(lead + (n,), F32)
    out = []
    for s in range(2):
        out += [zeros(na) if nope is None else nope[..., na * s:na * (s + 1)],
                zeros(ra) if rope is None else rope[..., ra * s:ra * (s + 1)],
                zeros(LANES // 2 - na - ra)]
    return out


def _w_in_layouts(w_in):
    w_t = jnp.swapaxes(w_in, 1, 2)
    depth, _, d = w_t.shape
    half, na, ra = HEAD_DIM // 2, B_NOPE // 2, B_ROPE // 2
    zeros = lambda n: jnp.zeros((depth, n, d), F32)

    def split_half(c):
        return [w_t[:, c:c + half], w_t[:, c + HEAD_DIM:c + HEAD_DIM + half],
                w_t[:, c + half:c + HEAD_DIM], w_t[:, c + HEAD_DIM + half:c + 2 * HEAD_DIM]]

    kbr = w_t[:, 896:928]
    pad = LANES // 2 - na - ra
    rope_key = [zeros(na), kbr[:, :ra], zeros(pad), zeros(na), kbr[:, ra:], zeros(pad)]
    small = (split_half(0) + split_half(128) + split_half(256) + [w_t[:, 384:896]] + rope_key
             + split_half(928) + split_half(1056) + split_half(1184) + [w_t[:, 1312:1696]])
    return jnp.concatenate(small, axis=1).astype(BF16), w_t[:, 1696:].astype(BF16)


def _stacked_weights(mix_pre_g, mix_post_g, ffn_pre_g, ffn_post_g, w_in, a_qn_g, a_kn_g, b_qa_g, b_kva_g,
                     b_w_qb, b_w_kvb, d_w_pool, d_scale, w_branch, w_out, w_ffn1, w_ffn3, w_ffn2):
    depth, d, _ = w_in.shape
    zeros = lambda rows, n: jnp.zeros((depth, rows, n), F32)
    half = HEAD_DIM // 2
    nope_rope = lambda nope, rope, rows: _nope_rope_cols(nope, rope, (depth, rows))
    w_small, w_gl = _w_in_layouts(w_in)
    per_q = B_NOPE + B_ROPE
    per_kv = B_NOPE + B_V
    qcols, kvcols = [], []
    for hd in range(B_HEADS):
        qcols += nope_rope(b_w_qb[:, :, per_q * hd:per_q * hd + B_NOPE],
                           b_w_qb[:, :, per_q * hd + B_NOPE:per_q * (hd + 1)], B_Q_RANK)
        kvcols += nope_rope(b_w_kvb[:, :, per_kv * hd:per_kv * hd + B_NOPE], None, B_KV_RANK)
        kvcols += [b_w_kvb[:, :, per_kv * hd + B_NOPE:per_kv * (hd + 1)], zeros(B_KV_RANK, LANES - B_V)]
    pool_rows = [jnp.concatenate([zeros(POOL_DIM, POOL_DIM * g), d_w_pool[:, g],
                                  zeros(POOL_DIM, POOL_DIM * (POOL_GROUPS - 1 - g))], axis=2)
                 for g in range(POOL_GROUPS)]
    gain = lambda g: g[:, None, :]
    head_gain = lambda g: jnp.concatenate([g[:, :half], g[:, :half], g[:, half:], g[:, half:]], axis=1)
    return {
        "w_small": w_small,
        "w_gl": w_gl,
        "wqb": jnp.concatenate(qcols, axis=2).astype(BF16),
        "wkvb": jnp.concatenate(kvcols, axis=2).astype(BF16),
        "wpool": jnp.concatenate(pool_rows, axis=1).astype(BF16),
        "w_branch": w_branch.reshape(depth, N_BRANCH * BRANCH_WIDTH, d).astype(BF16),
        "w_out": w_out.astype(BF16),
        "w_ffn1": w_ffn1.astype(BF16), "w_ffn3": w_ffn3.astype(BF16), "w_ffn2": w_ffn2.astype(BF16),
        "mix_pre_g": gain(mix_pre_g), "mix_post_g": gain(mix_post_g),
        "ffn_pre_g": gain(ffn_pre_g), "ffn_post_g": gain(ffn_post_g),
        "a_qn_g": gain(head_gain(a_qn_g)), "a_kn_g": gain(head_gain(a_kn_g)),
        "b_qa_g": gain(b_qa_g), "b_kva_g": gain(b_kva_g), "d_scale": gain(d_scale),
    }


def kernel(x, c, ctx, c_ctx, ada_w, ada_b, mix_pre_g, mix_post_g, ffn_pre_g, ffn_post_g, w_in, a_qn_g, a_kn_g,
           b_qa_g, b_kva_g, b_w_qb, b_w_kvb, c_sink, d_w_pool, d_scale, w_branch, w_out, w_ffn1, w_ffn3, w_ffn2):
    bsz, n_tok, d = x.shape
    depth = ada_w.shape[0]
    assert d == D_MODEL and ctx.shape[1] == CTX_LEN and bsz + 1 <= MOD_ROWS
    geo = _Rows(bsz, n_tok)

    cs = jnp.zeros((MOD_ROWS, d), F32).at[:bsz].set(c).at[bsz].set(c_ctx)
    mods = _mods_call(cs, ada_w, ada_b)[:, :, None, :]
    tabs = _rope_tables(n_tok)
    wts = _stacked_weights(mix_pre_g, mix_post_g, ffn_pre_g, ffn_post_g, w_in, a_qn_g, a_kn_g, b_qa_g, b_kva_g,
                           b_w_qb, b_w_kvb, d_w_pool, d_scale, w_branch, w_out, w_ffn1, w_ffn3, w_ffn2)

    x_lat, x_ctx = x, ctx
    for l in range(depth):
        with_ctx = l < depth - 1
        n_blocks = geo.all_blocks if with_ctx else geo.lat_blocks
        qa, kva, qb, kb, vb, qc, kvc, u = _pre_call(geo, l, x_lat, x_ctx, mods, wts, tabs)
        oa, ob = _dense_attn_call("attn_ab", geo, [qa, qb], [kva, kb, vb], _HEADS_AB, with_ctx)
        oc = _window_attn_call(geo, l, c_sink, qc, kvc, with_ctx)
        x1 = _merge_call(geo, l, n_blocks, x_lat, x_ctx, mods, [oa, ob, oc], u, wts)
        x_lat = x_ctx = _ffn_call(geo, l, x1, mods, wts)
    return x_lat.reshape(bsz, n_tok, d)
```

```python
import functools

import jax
import jax.numpy as jnp
from jax import lax
from jax.experimental import pallas as pl
from jax.experimental.pallas import tpu as pltpu

D_MODEL = 1024
CTX_LEN = 256
GRID_W = 64
HEAD_DIM = 64
ROPE_THETA = 10000.0
EPS = 1e-6
N_BRANCH = 4
BRANCH_WIDTH = 256
B_HEADS = 4
B_Q_RANK = 256
B_KV_RANK = 128
B_NOPE = 64
B_ROPE = 32
B_V = 64
WINDOW = 128
POOL_WINDOWS = (2, 4, 8, 16)
POOL_GROUPS = 4
POOL_DIM = 64
LOG2E = 1.4426950408889634

LANES = 128
TM = CTX_LEN
N_SUB = 4
RB = N_SUB * TM
HALO = 8
BAND = 2 * TM
TQ = TM
N_ATTN_BUF = 4
STAGE_GAP = 1
N_SMALL = 1792
MOD_ROWS = 16
VMEM_LIMIT = 56 * 1024 * 1024

F32 = jnp.float32
BF16 = jnp.bfloat16


def _dot(a, b):
    return jnp.dot(a, b, preferred_element_type=F32)


def _dot_t(a, b):
    return lax.dot_general(a, b, (((1,), (1,)), ((), ())), preferred_element_type=F32)


def _rms(x, g):
    return x * lax.rsqrt(jnp.mean(x * x, axis=-1, keepdims=True) + EPS) * g


def _lane_lo(shape, width):
    lane = lax.broadcasted_iota(jnp.int32, shape, 1)
    return (lane % (2 * width)) < width


def _head0(shape):
    return _lane_lo(shape, HEAD_DIM // 2)


def _rope(x, cos, sin):
    return x * cos + pltpu.roll(x, LANES // 2, 1) * sin


def _head_rms(x, g):
    h0 = _head0(x.shape)
    s = x * x
    s0 = jnp.sum(jnp.where(h0, s, 0.0), axis=-1, keepdims=True)
    s1 = jnp.sum(jnp.where(h0, 0.0, s), axis=-1, keepdims=True)
    ms = jnp.where(h0, s0, s1) * (1.0 / HEAD_DIM)
    return x * lax.rsqrt(ms + EPS) * g


def _pair_layouts(k, v):
    h0 = _head0(k.shape)
    lo = _lane_lo(v.shape, HEAD_DIM)
    k_up = pltpu.roll(k, HEAD_DIM // 2, 1)
    k_dn = pltpu.roll(k, LANES - HEAD_DIM // 2, 1)
    vr = pltpu.roll(v, HEAD_DIM, 1)
    return (jnp.where(h0, k, k_up), jnp.where(lo, v, 1.0), jnp.where(h0, k_dn, k), jnp.where(lo, vr, 1.0))


def _resident(arr, layer=None):
    if layer is None:
        nd = arr.ndim
        return pl.BlockSpec(arr.shape, lambda *_: (0,) * nd, pipeline_mode=pl.Buffered(1))
    nd = arr.ndim - 1
    return pl.BlockSpec((None,) + arr.shape[1:], lambda *_: (layer,) + (0,) * nd, pipeline_mode=pl.Buffered(1))


def _params(n_axes=1):
    return pltpu.CompilerParams(dimension_semantics=("arbitrary",) * n_axes, vmem_limit_bytes=VMEM_LIMIT)


class _Rows:
    def __init__(self, bsz, n_tok):
        assert n_tok % RB == 0 and (bsz * CTX_LEN) % RB == 0
        self.bsz, self.n_tok = bsz, n_tok
        self.n_lat = bsz * n_tok
        self.n_all = self.n_lat + bsz * CTX_LEN
        self.per_b = n_tok // RB
        self.lat_blocks = self.n_lat // RB
        self.all_blocks = self.n_all // RB

    def rows(self, width):
        return pl.BlockSpec((RB, width), lambda i: (i, 0))

    def lat_src(self, arr):
        last = self.lat_blocks - 1
        if arr.ndim == 3:
            return pl.BlockSpec((None, RB, arr.shape[2]),
                                lambda i: (jnp.minimum(i, last) // self.per_b, jnp.minimum(i, last) % self.per_b, 0))
        return pl.BlockSpec((RB, arr.shape[1]), lambda i: (jnp.minimum(i, last), 0))

    def ctx_src(self, arr):
        ctx_last = self.all_blocks - self.lat_blocks - 1
        rel = lambda i: jnp.clip(i - self.lat_blocks, 0, ctx_last)
        if arr.ndim == 3:
            return pl.BlockSpec((N_SUB, CTX_LEN, arr.shape[2]), lambda i: (rel(i), 0, 0))
        base = (arr.shape[0] - self.bsz * CTX_LEN) // RB
        return pl.BlockSpec((RB, arr.shape[1]), lambda i: (base + rel(i), 0))

    def mod(self, layer):
        return pl.BlockSpec((None, None, 1, 6 * D_MODEL),
                            lambda i: (layer, jnp.where(i < self.lat_blocks, i // self.per_b, self.bsz), 0, 0))

    def table(self):
        return pl.BlockSpec((RB, LANES), lambda i: (jnp.where(i < self.lat_blocks, i % self.per_b, self.per_b), 0))


def _mods_kernel(c_ref, w_ref, b_ref, o_ref):
    c = c_ref[...]
    s = c / (1.0 + jnp.exp(-c))
    o_ref[...] = _dot(s.astype(BF16), w_ref[...].astype(BF16)) + b_ref[...]


def _mods_call(cs, ada_w, ada_b):
    depth, d, n = ada_w.shape
    nb = 1536
    return pl.pallas_call(
        _mods_kernel,
        grid=(depth, n // nb),
        in_specs=[pl.BlockSpec((MOD_ROWS, d), lambda l, j: (0, 0)),
                  pl.BlockSpec((None, d, nb), lambda l, j: (l, 0, j)),
                  pl.BlockSpec((None, 1, nb), lambda l, j: (l, 0, j))],
        out_specs=pl.BlockSpec((None, MOD_ROWS, nb), lambda l, j: (l, 0, j)),
        out_shape=jax.ShapeDtypeStruct((depth, MOD_ROWS, n), F32),
        compiler_params=_params(2),
        name="adaln_mods",
    )(cs, ada_w, ada_b.reshape(depth, 1, n))


def _sub_rows(ref, sub):
    return ref[sub] if len(ref.shape) == 3 else ref[TM * sub:TM * (sub + 1), :]


def _pre_kernel(xl_ref, xc_ref, mod_ref, g_ref, w_ref, gq_ref, gk_ref, gqa_ref, gkva_ref, wqb_ref, wkvb_ref,
                cosa_ref, sina_ref, cosb_ref, sinb_ref,
                qa_ref, kva_ref, qb_ref, kb_ref, vb_ref, qc_ref, kvc_ref, u_ref, *, lat_blocks):
    d = D_MODEL
    is_ctx = pl.program_id(0) >= lat_blocks
    sc_a = HEAD_DIM ** -0.5 * LOG2E
    sc_b = (B_NOPE + B_ROPE) ** -0.5 * LOG2E
    for sub in range(N_SUB):
        rows = slice(TM * sub, TM * (sub + 1))
        x = jnp.where(is_ctx, _sub_rows(xc_ref, sub), _sub_rows(xl_ref, sub))
        h = _rms(x, g_ref[...]) * (1.0 + mod_ref[:, d:2 * d]) + mod_ref[:, 0:d]
        z = _dot_t(h.astype(BF16), w_ref[...])
        cosa, sina = cosa_ref[rows, :], sina_ref[rows, :]
        cosb, sinb = cosb_ref[rows, :], sinb_ref[rows, :]

        for j in range(2):
            q = _head_rms(z[:, LANES * j:LANES * (j + 1)], gq_ref[...])
            qa_ref[rows, LANES * j:LANES * (j + 1)] = (_rope(q, cosa, sina) * sc_a).astype(BF16)
        k = _rope(_head_rms(z[:, 256:384], gk_ref[...]), cosa, sina)
        for i, blk in enumerate(_pair_layouts(k, z[:, 384:512])):
            kva_ref[rows, LANES * i:LANES * (i + 1)] = blk.astype(BF16)

        qn = _rms(z[:, 512:768], gqa_ref[...]).astype(BF16)
        qb = _dot(qn, wqb_ref[...]) * sc_b
        kvn = _rms(z[:, 768:896], gkva_ref[...]).astype(BF16)
        kvb = _dot(kvn, wkvb_ref[...])
        kbr = _rope(z[:, 896:1024], cosb, sinb)
        lo = _lane_lo(kbr.shape, HEAD_DIM)
        for hd in range(B_HEADS):
            c0 = LANES * hd
            qb_ref[rows, c0:c0 + LANES] = _rope(qb[:, c0:c0 + LANES], cosb, sinb).astype(BF16)
            kb_ref[rows, c0:c0 + LANES] = (kvb[:, 2 * c0:2 * c0 + LANES] + kbr).astype(BF16)
            vb_ref[rows, c0:c0 + LANES] = jnp.where(lo, kvb[:, 2 * c0 + LANES:2 * c0 + 2 * LANES], 1.0).astype(BF16)

        for j in range(2):
            q = z[:, 1024 + LANES * j:1024 + LANES * (j + 1)]
            qc_ref[rows, LANES * j:LANES * (j + 1)] = (_rope(q, cosa, sina) * sc_a).astype(BF16)
        k = _rope(z[:, 1280:1408], cosa, sina)
        for i, blk in enumerate(_pair_layouts(k, z[:, 1408:1536])):
            kvc_ref[rows, LANES * i:LANES * (i + 1)] = blk.astype(BF16)

        u_ref[rows, :] = z[:, 1536:1792]


def _pre_call(geo, l, x_lat, x_ctx, mods, wts, tabs):
    widths = (256, 512, 512, 512, 512, 256, 512)
    out_shape = [jax.ShapeDtypeStruct((geo.n_all, w), BF16) for w in widths]
    out_shape.append(jax.ShapeDtypeStruct((geo.n_all, 256), F32))
    consts = [wts[k] for k in ("mix_pre_g", "w_small", "a_qn_g", "a_kn_g", "b_qa_g", "b_kva_g", "wqb", "wkvb")]
    return pl.pallas_call(
        functools.partial(_pre_kernel, lat_blocks=geo.lat_blocks),
        grid=(geo.all_blocks,),
        in_specs=[geo.lat_src(x_lat), geo.ctx_src(x_ctx), geo.mod(l)] + [_resident(a, l) for a in consts]
                 + [geo.table()] * 4,
        out_specs=[geo.rows(w) for w in widths] + [geo.rows(256)],
        out_shape=out_shape,
        compiler_params=_params(),
        name="pre_attn",
    )(x_lat, x_ctx, mods, *consts, *tabs)


def _probs(scores, extra=None):
    m = functools.reduce(jnp.maximum, [jnp.max(s, axis=-1, keepdims=True) for s in scores])
    if extra is not None:
        m = jnp.maximum(m, extra)
    ps = [jnp.exp2(s - m).astype(BF16) for s in scores]
    return ps, (None if extra is None else jnp.exp2(extra - m))


def _softmax_rows(s_ref, p_ref, extra=None):
    n = s_ref.shape[1]
    m = s_ref[:, 0:LANES]
    for c in range(1, n // LANES):
        m = jnp.maximum(m, s_ref[:, LANES * c:LANES * (c + 1)])
    m = jnp.max(m, axis=-1, keepdims=True)
    if extra is not None:
        m = jnp.maximum(m, extra)
    mb = jnp.broadcast_to(m, (s_ref.shape[0], LANES))
    for c in range(n // (2 * LANES)):
        cols = slice(2 * LANES * c, 2 * LANES * (c + 1))
        s = s_ref[:, cols]
        p_ref[:, cols] = jnp.concatenate([jnp.exp2(s[:, :LANES] - mb), jnp.exp2(s[:, LANES:] - mb)],
                                         axis=1).astype(BF16)
    return None if extra is None else jnp.exp2(extra - m)


def _normalise_pair(r0, r1, e0=None, e1=None):
    lo = _lane_lo(r0.shape, HEAD_DIM)
    r0r = pltpu.roll(r0, HEAD_DIM, 1)
    r1r = pltpu.roll(r1, HEAD_DIM, 1)
    den0 = r0r if e0 is None else r0r + e0
    den1 = r1 if e1 is None else r1 + e1
    return jnp.where(lo, r0 / den0, r1r / den1)


def _stack_pair(q):
    h0 = _head0(q.shape)
    zero = jnp.zeros_like(q)
    return jnp.concatenate([jnp.where(h0, q, zero), jnp.where(h0, zero, q)], axis=0)


def _dense_attn_kernel(*refs, heads, n_q, n_kv, with_ctx):
    q_lats, q_ctxs = refs[0:2 * n_q:2], refs[1:2 * n_q:2]
    kv_lat = refs[2 * n_q:2 * (n_q + n_kv):2]
    kv_ctx = refs[2 * n_q + 1:2 * (n_q + n_kv):2]
    outs = refs[2 * (n_q + n_kv):2 * (2 * n_q + n_kv)]
    o_lats, o_ctxs = outs[0::2], outs[1::2]
    scratch = refs[2 * (2 * n_q + n_kv):]
    n_heads = len(heads)
    s_bufs, p_bufs = scratch[0:N_ATTN_BUF], scratch[N_ATTN_BUF:2 * N_ATTN_BUF]
    r_bufs = scratch[2 * N_ATTN_BUF:]
    n_tok = q_lats[0].shape[0]
    n_tiles = n_tok // TQ

    def q_block(q_refs, row0, h, rows=TQ):
        qarr, qcol, mask, *_ = heads[h]
        q = q_refs[qarr][pl.ds(row0, rows), LANES * qcol:LANES * (qcol + 1)]
        if mask is None:
            return q
        h0 = _head0(q.shape)
        return jnp.where(h0 if mask == "head0" else ~h0, q, jnp.zeros_like(q))

    def keys(refs_, h):
        karr, kcol = heads[h][3:5]
        return refs_[karr][:, LANES * kcol:LANES * (kcol + 1)]

    def values(refs_, h):
        varr, vcol = heads[h][5:7]
        return refs_[varr][:, LANES * vcol:LANES * (vcol + 1)]

    def store_pair(o_refs, row0, pair, r0, r1):
        j = pair % 2
        o_refs[pair // 2][pl.ds(row0, r0.shape[0]), LANES * j:LANES * (j + 1)] = (
            _normalise_pair(r0, r1).astype(BF16))

    if with_ctx:
        rs = []
        for h in range(n_heads):
            (p,), _ = _probs([_dot_t(q_block(q_ctxs, 0, h, CTX_LEN), keys(kv_ctx, h))])
            rs.append(_dot(p, values(kv_ctx, h)))
        for pair in range(n_heads // 2):
            store_pair(o_ctxs, 0, pair, rs[2 * pair], rs[2 * pair + 1])

    def scores(row0, h, buf):
        q = q_block(q_lats, row0, h)
        s_bufs[buf][:, 0:CTX_LEN] = _dot_t(q, keys(kv_ctx, h))
        s_bufs[buf][:, CTX_LEN:] = _dot_t(q, keys(kv_lat, h))

    def softmax(buf):
        _softmax_rows(s_bufs[buf], p_bufs[buf])

    def weighted(h, buf):
        varr, vcol = heads[h][5:7]
        base = vcol // 2 * 2
        pair = slice(LANES * base, LANES * (base + 2))
        r2 = (_dot(p_bufs[buf][:, 0:CTX_LEN], kv_ctx[varr][:, pair])
              + _dot(p_bufs[buf][:, CTX_LEN:], kv_lat[varr][:, pair]))
        return r2[:, LANES * (vcol - base):LANES * (vcol - base + 1)]

    assert n_heads % N_ATTN_BUF == 0 and 2 * STAGE_GAP < N_ATTN_BUF
    buf_of = lambda h: h % N_ATTN_BUF
    for h in range(n_heads - STAGE_GAP, n_heads):
        p_bufs[buf_of(h)][...] = jnp.zeros(p_bufs[0].shape, BF16)
    for rb in r_bufs:
        rb[...] = jnp.zeros(rb.shape, F32)
    for h in range(STAGE_GAP):
        scores(0, h, buf_of(h))

    def finish(row0, hp):
        r = weighted(hp, buf_of(hp))
        pair = hp // 2
        if hp % 2 == 0:
            r_bufs[pair % len(r_bufs)][...] = r
        else:
            store_pair(o_lats, row0, pair, r_bufs[pair % len(r_bufs)][...], r)

    def tile_body(i, carry):
        row0 = pl.multiple_of(i * TQ, TQ)
        prev0 = pl.multiple_of(jnp.maximum(i - 1, 0) * TQ, TQ)
        next0 = pl.multiple_of(jnp.minimum(i + 1, n_tiles - 1) * TQ, TQ)
        for h in range(n_heads):
            ahead = h + STAGE_GAP
            scores(row0 if ahead < n_heads else next0, ahead % n_heads, buf_of(ahead))
            softmax(buf_of(h))
            behind = h - STAGE_GAP
            finish(row0 if behind >= 0 else prev0, behind % n_heads)
        return carry

    lax.fori_loop(0, n_tiles, tile_body, 0)
    for h in range(n_heads - STAGE_GAP, n_heads):
        finish(n_tok - TQ, h)
    if not with_ctx:
        for o_ctx in o_ctxs:
            o_ctx[...] = jnp.zeros(o_ctx.shape, BF16)


_HEADS_AB = (tuple((0, h // 2, "head0" if h % 2 == 0 else "head1", 0, 2 * (h // 2), 0, 2 * (h // 2) + 1)
                   for h in range(4))
             + tuple((1, h, None, 1, h, 2, h) for h in range(4)))


def _batch_specs(geo, width):
    ctx0 = geo.n_lat // CTX_LEN
    return [pl.BlockSpec((geo.n_tok, width), lambda b: (b, 0)),
            pl.BlockSpec((CTX_LEN, width), lambda b: (ctx0 + b, 0))]


def _attn_out(geo):
    specs = [pl.BlockSpec((geo.n_tok, BRANCH_WIDTH), lambda b: (b, 0)),
             pl.BlockSpec((CTX_LEN, BRANCH_WIDTH), lambda b: (b, 0))]
    shapes = [jax.ShapeDtypeStruct((geo.n_lat, BRANCH_WIDTH), BF16),
              jax.ShapeDtypeStruct((geo.bsz * CTX_LEN, BRANCH_WIDTH), BF16)]
    return specs, shapes


def _dense_attn_call(name, geo, qs, kvs, heads, with_ctx):
    s_len = geo.n_tok + CTX_LEN
    in_specs, args = [], []
    for arr in list(qs) + list(kvs):
        in_specs += _batch_specs(geo, arr.shape[1])
        args += [arr, arr]
    out_specs, out_shape = [], []
    for _ in qs:
        specs, shapes = _attn_out(geo)
        out_specs += specs
        out_shape += shapes
    outs = pl.pallas_call(
        functools.partial(_dense_attn_kernel, heads=heads, n_q=len(qs), n_kv=len(kvs), with_ctx=with_ctx),
        grid=(geo.bsz,),
        in_specs=in_specs,
        out_specs=out_specs,
        out_shape=out_shape,
        scratch_shapes=[pltpu.VMEM((TQ, s_len), F32)] * N_ATTN_BUF + [pltpu.VMEM((TQ, s_len), BF16)] * N_ATTN_BUF
                       + [pltpu.VMEM((TQ, LANES), F32)] * 2,
        compiler_params=_params(),
        name=name,
    )(*args)
    return [outs[2 * m:2 * m + 2] for m in range(len(qs))]


def _window_attn_kernel(sink_ref, q_lat, q_ctx, kv_lat, kv_ctx, o_lat, o_ctx, *scratch, with_ctx, layer):
    s_bufs, p_bufs, e_bufs, bias_ref = scratch[0:4], scratch[4:8], scratch[8:12], scratch[12]
    n_tok = q_lat.shape[0]
    n_tiles = n_tok // TM
    assert n_tiles >= 2

    @pl.when(pl.program_id(0) == 0)
    def _():
        row = lax.broadcasted_iota(jnp.int32, (2 * TM, BAND), 0)
        col = lax.broadcasted_iota(jnp.int32, (2 * TM, BAND), 1)
        rel0 = col - jnp.where(row < TM, row, row - TM)
        for place, off in enumerate((0, -WINDOW, TM - BAND)):
            bias_ref[place] = jnp.where(jnp.abs(rel0 + off) <= WINDOW, 0.0, -1e30)

    def sink_col(j):
        row = lax.broadcasted_iota(jnp.int32, (2 * TM, 1), 0)
        return jnp.where(row < TM, sink_ref[layer, 2 * j], sink_ref[layer, 2 * j + 1]) * LOG2E

    def cols(j):
        return slice(2 * LANES * j, 2 * LANES * j + LANES), slice(2 * LANES * j + LANES, 2 * LANES * (j + 1))

    def store_pair(ref, row0, j, r, e):
        ref[pl.ds(row0, TM), LANES * j:LANES * (j + 1)] = _normalise_pair(r[:TM], r[TM:], e[:TM], e[TM:]).astype(BF16)

    if with_ctx:
        for j in range(2):
            c_kk, c_v1 = cols(j)
            (p,), e = _probs([_dot_t(_stack_pair(q_ctx[:, LANES * j:LANES * (j + 1)]), kv_ctx[:, c_kk])], sink_col(j))
            store_pair(o_ctx, 0, j, _dot(p, kv_ctx[:, c_v1]), e)

    def band_start(i):
        if isinstance(i, int):
            return min(max(i * TM - WINDOW, 0), n_tok - BAND)
        return pl.multiple_of(jnp.clip(i * TM - WINDOW, 0, n_tok - BAND), WINDOW)

    def scores(i, j, buf):
        c_kk, _ = cols(j)
        row0 = i * TM if isinstance(i, int) else pl.multiple_of(i * TM, TM)
        qs = _stack_pair(q_lat[pl.ds(row0, TM), LANES * j:LANES * (j + 1)])
        s_bufs[buf][:, 0:CTX_LEN] = _dot_t(qs, kv_ctx[:, c_kk])
        if isinstance(i, int):
            place = 0 if i == 0 else 2 if i == n_tiles - 1 else 1
        else:
            place = jnp.where(i == 0, 0, jnp.where(i == n_tiles - 1, 2, 1))
        s_bufs[buf][:, CTX_LEN:] = _dot_t(qs, kv_lat[pl.ds(band_start(i), BAND), c_kk]) + bias_ref[place]

    def softmax(i, j, buf):
        e = _softmax_rows(s_bufs[buf], p_bufs[buf], sink_col(j))
        e_bufs[buf][...] = jnp.broadcast_to(e, e_bufs[buf].shape)

    def weighted(i, j, buf):
        both = slice(2 * LANES * j, 2 * LANES * (j + 1))
        r2 = (_dot(p_bufs[buf][:, 0:CTX_LEN], kv_ctx[:, both])
              + _dot(p_bufs[buf][:, CTX_LEN:], kv_lat[pl.ds(band_start(i), BAND), both]))
        return r2[:, LANES:]

    def finish(i, j, buf):
        row0 = i * TM if isinstance(i, int) else pl.multiple_of(i * TM, TM)
        store_pair(o_lat, row0, j, weighted(i, j, buf), e_bufs[buf][...])

    def pair_of_tiles(i0, first, last):
        i1 = i0 + 1
        for c, (i, j) in enumerate(((i0, 0), (i0, 1), (i1, 0), (i1, 1))):
            if c < 2:
                scores(i1, j, c + 2)
            elif not last:
                scores(i0 + 2, j, c - 2)
            softmax(i, j, c)
            if c >= 2:
                finish(i0, j, c - 2)
            elif not first:
                finish(i0 - 1, j, c + 2)

    assert n_tiles % 2 == 0 and n_tiles >= 4
    scores(0, 0, 0)
    scores(0, 1, 1)
    pair_of_tiles(0, True, False)

    def body(k, carry):
        pair_of_tiles(2 * k, False, False)
        return carry

    lax.fori_loop(1, n_tiles // 2 - 1, body, 0)
    pair_of_tiles(n_tiles - 2, False, True)
    finish(n_tiles - 1, 0, 2)
    finish(n_tiles - 1, 1, 3)
    if not with_ctx:
        o_ctx[...] = jnp.zeros(o_ctx.shape, BF16)


def _window_attn_call(geo, l, sink, q, kv, with_ctx):
    n_keys = CTX_LEN + BAND
    out_specs, out_shape = _attn_out(geo)
    return pl.pallas_call(
        functools.partial(_window_attn_kernel, with_ctx=with_ctx, layer=l),
        grid=(geo.bsz,),
        in_specs=[pl.BlockSpec(memory_space=pltpu.SMEM)] + _batch_specs(geo, q.shape[1]) + _batch_specs(geo, kv.shape[1]),
        out_specs=out_specs,
        out_shape=out_shape,
        scratch_shapes=[pltpu.VMEM((2 * TM, n_keys), F32)] * 4 + [pltpu.VMEM((2 * TM, n_keys), BF16)] * 4
                       + [pltpu.VMEM((2 * TM, LANES), F32)] * 4 + [pltpu.VMEM((3, 2 * TM, BAND), F32)],
        compiler_params=_params(),
        name="attn_c",
    )(sink, q, q, kv, kv)


def _window_sums(ext):
    assert POOL_WINDOWS == (2, 4, 8, 16) and HALO >= 8
    n = ext.shape[0]
    back = lambda x, k: pltpu.roll(x, k, 0)
    ahead = lambda x, k: pltpu.roll(x, n - k, 0)
    e0, e1 = ext[:, 0:LANES], ext[:, LANES:2 * LANES]
    lo = _lane_lo(e0.shape, POOL_DIM)
    a2 = e0 + back(e0, 1)
    a4 = a2 + back(a2, 2)
    col0 = jnp.where(lo, a2, ahead(a4, 1))
    b2 = e1 + back(e1, 1)
    b4 = b2 + back(b2, 2)
    b8 = b4 + back(b4, 4)
    b16 = b8 + back(b8, 8)
    col1 = jnp.where(lo, ahead(b8, 3), ahead(b16, 7))
    return jnp.concatenate([col0[HALO:HALO + TM], col1[HALO:HALO + TM]], axis=1)


def _merge_kernel(xl_ref, xc_ref, mod_ref, oal_ref, oac_ref, obl_ref, obc_ref, ocl_ref, occ_ref,
                  u_ref, up_ref, un_ref, gpre_ref, wgl_ref, wpool_ref, dscale_ref,
                  wbr_ref, wout_ref, gpost_ref, out_ref, *, lat_blocks, per_b):
    d = D_MODEL
    i = pl.program_id(0)
    is_ctx = i >= lat_blocks
    blk = i % per_b
    lane = lax.broadcasted_iota(jnp.int32, (TM, 2 * LANES), 1)
    grp = lane // POOL_DIM
    row = lax.broadcasted_iota(jnp.int32, (TM, 2 * LANES), 0)
    p_lo = jnp.left_shift(1, grp)
    n_tok = jnp.where(is_ctx, CTX_LEN, per_b * RB)
    zero_halo = jnp.zeros((HALO, 2 * LANES), F32)

    for sub in range(N_SUB):
        rows = slice(TM * sub, TM * (sub + 1))
        x = jnp.where(is_ctx, _sub_rows(xc_ref, sub), _sub_rows(xl_ref, sub))
        hb = (_rms(x, gpre_ref[...]) * (1.0 + mod_ref[:, d:2 * d]) + mod_ref[:, 0:d]).astype(BF16)

        u = u_ref[rows, :]
        if sub == 0:
            prev = jnp.where(jnp.logical_or(is_ctx, blk == 0), zero_halo, up_ref[...])
        else:
            prev = jnp.where(is_ctx, zero_halo, u_ref[TM * sub - HALO:TM * sub, :])
        if sub == N_SUB - 1:
            nxt = jnp.where(jnp.logical_or(is_ctx, blk == per_b - 1), zero_halo, un_ref[...])
        else:
            nxt = jnp.where(is_ctx, zero_halo, u_ref[TM * (sub + 1):TM * (sub + 1) + HALO, :])
        sums = _window_sums(jnp.concatenate([prev, u, nxt], axis=0))
        pos = jnp.where(is_ctx, row, blk * RB + TM * sub + row)
        cnt = jnp.minimum(pos + p_lo, n_tok) - jnp.maximum(pos - p_lo, 0)
        diff = sums / cnt.astype(F32) - u
        o_d = _dot(diff.astype(BF16), wpool_ref[...]) * dscale_ref[...]

        branches = (jnp.where(is_ctx, oac_ref[rows, :], oal_ref[rows, :]),
                    jnp.where(is_ctx, obc_ref[rows, :], obl_ref[rows, :]),
                    jnp.where(is_ctx, occ_ref[rows, :], ocl_ref[rows, :]),
                    o_d.astype(BF16))
        acc = jnp.zeros((TM, d), F32)
        for k in range(N_BRANCH):
            gl = _dot_t(hb, wgl_ref[d * k:d * (k + 1), :])
            proj = _dot(branches[k], wbr_ref[BRANCH_WIDTH * k:BRANCH_WIDTH * (k + 1), :])
            acc = acc + proj / (1.0 + jnp.exp(-gl))
        y = _dot(acc.astype(BF16), wout_ref[...])
        out_ref[rows, :] = x + mod_ref[:, 2 * d:3 * d] * _rms(y, gpost_ref[...])


def _merge_call(geo, l, n_blocks, x_lat, x_ctx, mods, o_pairs, u, wts):
    r = RB // HALO
    n_halo = geo.n_all // HALO
    up_spec = pl.BlockSpec((HALO, 256), lambda i: (jnp.maximum(i * r - 1, 0), 0))
    un_spec = pl.BlockSpec((HALO, 256), lambda i: (jnp.minimum((i + 1) * r, n_halo - 1), 0))
    o_specs, o_args = [], []
    for o_l, o_c in o_pairs:
        o_specs += [geo.lat_src(o_l), geo.ctx_src(o_c)]
        o_args += [o_l, o_c]
    consts = [wts[k] for k in ("mix_pre_g", "w_gl", "wpool", "d_scale", "w_branch", "w_out", "mix_post_g")]
    return pl.pallas_call(
        functools.partial(_merge_kernel, lat_blocks=geo.lat_blocks, per_b=geo.per_b),
        grid=(n_blocks,),
        in_specs=[geo.lat_src(x_lat), geo.ctx_src(x_ctx), geo.mod(l)] + o_specs
                 + [geo.rows(256), up_spec, un_spec] + [_resident(a, l) for a in consts],
        out_specs=geo.rows(D_MODEL),
        out_shape=jax.ShapeDtypeStruct((n_blocks * RB, D_MODEL), F32),
        compiler_params=_params(),
        name="merge",
    )(x_lat, x_ctx, mods, *o_args, u, u, u, *consts)


def _ffn_kernel(x_ref, mod_ref, gpre_ref, w1_ref, w3_ref, w2_ref, gpost_ref, out_ref):
    d = D_MODEL
    for sub in range(N_SUB):
        rows = slice(TM * sub, TM * (sub + 1))
        x = x_ref[rows, :]
        hb = (_rms(x, gpre_ref[...]) * (1.0 + mod_ref[:, 4 * d:5 * d]) + mod_ref[:, 3 * d:4 * d]).astype(BF16)
        a = _dot(hb, w1_ref[...])
        b = _dot(hb, w3_ref[...])
        act = (a / (1.0 + jnp.exp(-a)) * b).astype(BF16)
        f = _dot(act, w2_ref[...])
        out_ref[rows, :] = x + mod_ref[:, 5 * d:6 * d] * _rms(f, gpost_ref[...])


def _ffn_call(geo, l, x1, mods, wts):
    n, d = x1.shape
    consts = [wts[k] for k in ("ffn_pre_g", "w_ffn1", "w_ffn3", "w_ffn2", "ffn_post_g")]
    return pl.pallas_call(
        _ffn_kernel,
        grid=(n // RB,),
        in_specs=[geo.rows(d), geo.mod(l)] + [_resident(a, l) for a in consts],
        out_specs=geo.rows(d),
        out_shape=jax.ShapeDtypeStruct((n, d), F32),
        compiler_params=_params(),
        name="ffn",
    )(x1, mods, *consts)


def _rope_tables(n_tok):
    rows = n_tok // GRID_W
    row = jnp.repeat(jnp.arange(rows), GRID_W).astype(F32)
    col = jnp.tile(jnp.arange(GRID_W), rows).astype(F32)

    def cos_sin(rot_dim):
        n_freq = rot_dim // 4
        inv = ROPE_THETA ** (-jnp.arange(n_freq, dtype=F32) / n_freq)
        ang = jnp.concatenate([row[:, None] * inv, col[:, None] * inv], axis=-1)
        return jnp.cos(ang), jnp.sin(ang)

    c64, s64 = cos_sin(HEAD_DIM)
    cos_a = jnp.tile(c64, (1, 4))
    sin_a = jnp.concatenate([-s64, -s64, s64, s64], axis=-1)
    c32, s32 = cos_sin(B_ROPE)
    ones = lambda n: jnp.ones((n_tok, n), F32)
    zeros = lambda n: jnp.zeros((n_tok, n), F32)
    na, ra = B_NOPE // 2, B_ROPE // 2
    pad = LANES // 2 - na - ra
    cos_b = jnp.concatenate([ones(na), c32, ones(pad)] * 2, axis=-1)
    sin_b = jnp.concatenate([zeros(na), -s32, zeros(pad), zeros(na), s32, zeros(pad)], axis=-1)
    ident_c = jnp.ones((RB, LANES), F32)
    ident_s = jnp.zeros((RB, LANES), F32)
    return (jnp.concatenate([cos_a, ident_c]), jnp.concatenate([sin_a, ident_s]),
            jnp.concatenate([cos_b, ident_c]), jnp.concatenate([sin_b, ident_s]))


def _nope_rope_cols(nope, rope, lead):
    na, ra = B_NOPE // 2, B_ROPE // 2
    zeros = lambda n: jnp.zeros(lead + (n,), F32)
    out = []
    for s in range(2):
        out += [zeros(na) if nope is None else nope[..., na * s:na * (s + 1)],
                zeros(ra) if rope is None else rope[..., ra * s:ra * (s + 1)],
                zeros(LANES // 2 - na - ra)]
    return out


def _w_in_layouts(w_in):
    w_t = jnp.swapaxes(w_in, 1, 2)
    depth, _, d = w_t.shape
    half, na, ra = HEAD_DIM // 2, B_NOPE // 2, B_ROPE // 2
    zeros = lambda n: jnp.zeros((depth, n, d), F32)

    def split_half(c):
        return [w_t[:, c:c + half], w_t[:, c + HEAD_DIM:c + HEAD_DIM + half],
                w_t[:, c + half:c + HEAD_DIM], w_t[:, c + HEAD_DIM + half:c + 2 * HEAD_DIM]]

    kbr = w_t[:, 896:928]
    pad = LANES // 2 - na - ra
    rope_key = [zeros(na), kbr[:, :ra], zeros(pad), zeros(na), kbr[:, ra:], zeros(pad)]
    small = (split_half(0) + split_half(128) + split_half(256) + [w_t[:, 384:896]] + rope_key
             + split_half(928) + split_half(1056) + split_half(1184) + [w_t[:, 1312:1696]])
    return jnp.concatenate(small, axis=1).astype(BF16), w_t[:, 1696:].astype(BF16)


def _stacked_weights(mix_pre_g, mix_post_g, ffn_pre_g, ffn_post_g, w_in, a_qn_g, a_kn_g, b_qa_g, b_kva_g,
                     b_w_qb, b_w_kvb, d_w_pool, d_scale, w_branch, w_out, w_ffn1, w_ffn3, w_ffn2):
    depth, d, _ = w_in.shape
    zeros = lambda rows, n: jnp.zeros((depth, rows, n), F32)
    half = HEAD_DIM // 2
    nope_rope = lambda nope, rope, rows: _nope_rope_cols(nope, rope, (depth, rows))
    w_small, w_gl = _w_in_layouts(w_in)
    per_q = B_NOPE + B_ROPE
    per_kv = B_NOPE + B_V
    qcols, kvcols = [], []
    for hd in range(B_HEADS):
        qcols += nope_rope(b_w_qb[:, :, per_q * hd:per_q * hd + B_NOPE],
                           b_w_qb[:, :, per_q * hd + B_NOPE:per_q * (hd + 1)], B_Q_RANK)
        kvcols += nope_rope(b_w_kvb[:, :, per_kv * hd:per_kv * hd + B_NOPE], None, B_KV_RANK)
        kvcols += [b_w_kvb[:, :, per_kv * hd + B_NOPE:per_kv * (hd + 1)], zeros(B_KV_RANK, LANES - B_V)]
    pool_rows = [jnp.concatenate([zeros(POOL_DIM, POOL_DIM * g), d_w_pool[:, g],
                                  zeros(POOL_DIM, POOL_DIM * (POOL_GROUPS - 1 - g))], axis=2)
                 for g in range(POOL_GROUPS)]
    gain = lambda g: g[:, None, :]
    head_gain = lambda g: jnp.concatenate([g[:, :half], g[:, :half], g[:, half:], g[:, half:]], axis=1)
    return {
        "w_small": w_small,
        "w_gl": w_gl,
        "wqb": jnp.concatenate(qcols, axis=2).astype(BF16),
        "wkvb": jnp.concatenate(kvcols, axis=2).astype(BF16),
        "wpool": jnp.concatenate(pool_rows, axis=1).astype(BF16),
        "w_branch": w_branch.reshape(depth, N_BRANCH * BRANCH_WIDTH, d).astype(BF16),
        "w_out": w_out.astype(BF16),
        "w_ffn1": w_ffn1.astype(BF16), "w_ffn3": w_ffn3.astype(BF16), "w_ffn2": w_ffn2.astype(BF16),
        "mix_pre_g": gain(mix_pre_g), "mix_post_g": gain(mix_post_g),
        "ffn_pre_g": gain(ffn_pre_g), "ffn_post_g": gain(ffn_post_g),
        "a_qn_g": gain(head_gain(a_qn_g)), "a_kn_g": gain(head_gain(a_kn_g)),
        "b_qa_g": gain(b_qa_g), "b_kva_g": gain(b_kva_g), "d_scale": gain(d_scale),
    }


def kernel(x, c, ctx, c_ctx, ada_w, ada_b, mix_pre_g, mix_post_g, ffn_pre_g, ffn_post_g, w_in, a_qn_g, a_kn_g,
           b_qa_g, b_kva_g, b_w_qb, b_w_kvb, c_sink, d_w_pool, d_scale, w_branch, w_out, w_ffn1, w_ffn3, w_ffn2):
    bsz, n_tok, d = x.shape
    depth = ada_w.shape[0]
    assert d == D_MODEL and ctx.shape[1] == CTX_LEN and bsz + 1 <= MOD_ROWS
    geo = _Rows(bsz, n_tok)

    cs = jnp.zeros((MOD_ROWS, d), F32).at[:bsz].set(c).at[bsz].set(c_ctx)
    mods = _mods_call(cs, ada_w, ada_b)[:, :, None, :]
    tabs = _rope_tables(n_tok)
    wts = _stacked_weights(mix_pre_g, mix_post_g, ffn_pre_g, ffn_post_g, w_in, a_qn_g, a_kn_g, b_qa_g, b_kva_g,
                           b_w_qb, b_w_kvb, d_w_pool, d_scale, w_branch, w_out, w_ffn1, w_ffn3, w_ffn2)

    x_lat, x_ctx = x, ctx
    for l in range(depth):
        with_ctx = l < depth - 1
        n_blocks = geo.all_blocks if with_ctx else geo.lat_blocks
        qa, kva, qb, kb, vb, qc, kvc, u = _pre_call(geo, l, x_lat, x_ctx, mods, wts, tabs)
        oa, ob = _dense_attn_call("attn_ab", geo, [qa, qb], [kva, kb, vb], _HEADS_AB, with_ctx)
        oc = _window_attn_call(geo, l, c_sink, qc, kvc, with_ctx)
        x1 = _merge_call(geo, l, n_blocks, x_lat, x_ctx, mods, [oa, ob, oc], u, wts)
        x_lat = x_ctx = _ffn_call(geo, l, x1, mods, wts)
    return x_lat.reshape(bsz, n_tok, d)
```
